```python
import math
import jax, jax.numpy as jnp
from jax import lax
import numpy as np

D_MODEL = 2048
BATCH = 8
SEQ = 4096
DEPTH = 1
DEC_BATCH = 1
DEC_SEQ = 8192
PAST_LEN = 128

W_A = D_MODEL // 2
W_B = D_MODEL // 2
K_A = 3
K_B = 31
N_MEM = 256
N_XHEADS = 4
XHEAD_DIM = D_MODEL // N_XHEADS
D_FF = int(math.ceil((8 * D_MODEL / 3) / 256) * 256)
EPS = 1e-6
SPLIT_SIZES = (W_A, W_A, W_A, W_B, W_B, D_MODEL, D_MODEL)
SPLIT_POINTS = tuple(int(s) for s in np.cumsum(SPLIT_SIZES)[:-1])
D_IN = int(sum(SPLIT_SIZES))

kernel_name = "hybrid_conv_gated_encoder"


def rmsnorm(x, g):
    xf = x.astype(jnp.float32)
    y = xf * lax.rsqrt(jnp.mean(xf * xf, axis=-1, keepdims=True) + EPS)
    return (y * g.astype(jnp.float32)).astype(x.dtype)


def layernorm(x, g, b):
    xf = x.astype(jnp.float32)
    mu = jnp.mean(xf, axis=-1, keepdims=True)
    var = jnp.mean(jnp.square(xf - mu), axis=-1, keepdims=True)
    y = (xf - mu) * lax.rsqrt(var + EPS)
    return (y * g.astype(jnp.float32) + b.astype(jnp.float32)).astype(x.dtype)


def depthwise_conv(x, w):
    k, c = w.shape
    return lax.conv_general_dilated(
        x, w[:, None, :].astype(x.dtype), window_strides=(1,),
        padding=[(k // 2, k // 2)], dimension_numbers=("NWC", "WIO", "NWC"),
        feature_group_count=c)


def encoder_layer(x, mem, g_mix, w_in, conv_a_w, w_out_a, conv_b_w, conv_b_bias,
                  ln_b_g, ln_b_b, w_out_b, w_o, g_xattn, g_mem, w_q, w_kv, w_xo,
                  g_ffn, w_gate_up, w_down):
    bsz, seq, _ = x.shape
    xn = rmsnorm(x, g_mix)
    proj = xn @ w_in
    a_b, a_c, a_v, b_val, b_gate, gate_a, gate_b = jnp.split(proj, SPLIT_POINTS, axis=-1)
    y_a = (a_b * depthwise_conv(a_c * a_v, conv_a_w)) @ w_out_a
    u = b_val * jax.nn.sigmoid(b_gate)
    u = depthwise_conv(u, conv_b_w) + conv_b_bias
    u = jax.nn.silu(layernorm(u, ln_b_g, ln_b_b))
    y_b = u @ w_out_b
    merged = jax.nn.sigmoid(gate_a) * y_a + jax.nn.sigmoid(gate_b) * y_b
    x = x + merged @ w_o
    hn = rmsnorm(x, g_xattn)
    mn = rmsnorm(mem, g_mem)
    q = (hn @ w_q).reshape(bsz, seq, N_XHEADS, XHEAD_DIM)
    k, v = jnp.split(mn @ w_kv, 2, axis=-1)
    k = k.reshape(bsz, N_MEM, N_XHEADS, XHEAD_DIM)
    v = v.reshape(bsz, N_MEM, N_XHEADS, XHEAD_DIM)
    s = jnp.einsum("bqhd,bkhd->bhqk", q, k).astype(jnp.float32) * (XHEAD_DIM ** -0.5)
    p = jax.nn.softmax(s, axis=-1).astype(v.dtype)
    o = jnp.einsum("bhqk,bkhd->bqhd", p, v).reshape(bsz, seq, D_MODEL)
    x = x + o @ w_xo
    hn = rmsnorm(x, g_ffn)
    gt, up = jnp.split(hn @ w_gate_up, 2, axis=-1)
    x = x + (jax.nn.silu(gt) * up) @ w_down
    return x


def trunk(x, mem, g_mix, w_in, conv_a_w, w_out_a, conv_b_w, conv_b_bias, ln_b_g, ln_b_b,
          w_out_b, w_o, g_xattn, g_mem, w_q, w_kv, w_xo, g_ffn, w_gate_up, w_down, g_final):
    for l in range(DEPTH):
        x = encoder_layer(x, mem, g_mix[l], w_in[l], conv_a_w[l], w_out_a[l], conv_b_w[l],
                          conv_b_bias[l], ln_b_g[l], ln_b_b[l], w_out_b[l], w_o[l],
                          g_xattn[l], g_mem[l], w_q[l], w_kv[l], w_xo[l], g_ffn[l],
                          w_gate_up[l], w_down[l])
    return rmsnorm(x, g_final)


def setup_inputs(seed: int = 0) -> dict:
    key = jax.random.key(seed)
    ks = jax.random.split(key, 24)
    f32 = jnp.float32

    def nrm(k, shape, scale):
        return jax.random.normal(k, shape, f32) * scale

    def gain(k, shape):
        return 1.0 + 0.05 * jax.random.normal(k, shape, f32)

    L = DEPTH
    return {
        "x_prompt": nrm(ks[0], (BATCH, SEQ, D_MODEL), 1.0),
        "x_sample": nrm(ks[1], (DEC_BATCH, DEC_SEQ, D_MODEL), 1.0),
        "mem_prompt": nrm(ks[2], (BATCH, N_MEM, D_MODEL), 1.0),
        "mem_sample": nrm(ks[3], (DEC_BATCH, N_MEM, D_MODEL), 1.0),
        "g_mix": gain(ks[4], (L, D_MODEL)),
        "w_in": nrm(ks[5], (L, D_MODEL, D_IN), D_MODEL ** -0.5),
        "conv_a_w": nrm(ks[6], (L, K_A, W_A), K_A ** -0.5),
        "w_out_a": nrm(ks[7], (L, W_A, D_MODEL), W_A ** -0.5),
        "conv_b_w": nrm(ks[8], (L, K_B, W_B), K_B ** -0.5),
        "conv_b_bias": nrm(ks[9], (L, W_B), 0.02),
        "ln_b_g": gain(ks[10], (L, W_B)),
        "ln_b_b": nrm(ks[11], (L, W_B), 0.02),
        "w_out_b": nrm(ks[12], (L, W_B, D_MODEL), W_B ** -0.5),
        "w_o": nrm(ks[13], (L, D_MODEL, D_MODEL), D_MODEL ** -0.5),
        "g_xattn": gain(ks[14], (L, D_MODEL)),
        "g_mem": gain(ks[15], (L, D_MODEL)),
        "w_q": nrm(ks[16], (L, D_MODEL, D_MODEL), D_MODEL ** -0.5),
        "w_kv": nrm(ks[17], (L, D_MODEL, 2 * D_MODEL), D_MODEL ** -0.5),
        "w_xo": nrm(ks[18], (L, D_MODEL, D_MODEL), D_MODEL ** -0.5),
        "g_ffn": gain(ks[19], (L, D_MODEL)),
        "w_gate_up": nrm(ks[20], (L, D_MODEL, 2 * D_FF), D_MODEL ** -0.5),
        "w_down": nrm(ks[21], (L, D_FF, D_MODEL), D_FF ** -0.5),
        "g_final": gain(ks[22], (D_MODEL,)),
    }


def reference(x_prompt, x_sample, mem_prompt, mem_sample, g_mix, w_in, conv_a_w, w_out_a,
              conv_b_w, conv_b_bias, ln_b_g, ln_b_b, w_out_b, w_o, g_xattn, g_mem, w_q,
              w_kv, w_xo, g_ffn, w_gate_up, w_down, g_final):
    y_prompt = trunk(x_prompt, mem_prompt, g_mix, w_in, conv_a_w, w_out_a, conv_b_w,
                     conv_b_bias, ln_b_g, ln_b_b, w_out_b, w_o, g_xattn, g_mem, w_q, w_kv,
                     w_xo, g_ffn, w_gate_up, w_down, g_final)
    y_sample = trunk(x_sample, mem_sample, g_mix, w_in, conv_a_w, w_out_a, conv_b_w,
                     conv_b_bias, ln_b_g, ln_b_b, w_out_b, w_o, g_xattn, g_mem, w_q, w_kv,
                     w_xo, g_ffn, w_gate_up, w_down, g_final)
    return (y_prompt, y_sample)
```

```python
import functools

import jax
import jax.numpy as jnp
from jax import lax
from jax.experimental import pallas as pl
from jax.experimental.pallas import tpu as pltpu

EPS = 1e-6
N_XHEADS = 4
K_A = 3
K_B = 31
HALO = 16
V7X_VMEM_LIMIT_BYTES = 56 * 1024 * 1024

F32 = jnp.float32
BF16 = jnp.bfloat16


def _params(n_grid):
    return pltpu.CompilerParams(
        dimension_semantics=("arbitrary",) * n_grid,
        vmem_limit_bytes=V7X_VMEM_LIMIT_BYTES)


def _resident(shape, index_map):
    return pl.BlockSpec(shape, index_map, pipeline_mode=pl.Buffered(1))


def _rms(x, g):
    return x * lax.rsqrt(jnp.mean(x * x, axis=-1, keepdims=True) + EPS) * g


def _dot(a, b):
    return jnp.dot(a, b, preferred_element_type=F32)


def _in_proj_kernel(x_ref, g_ref, wab, wac, wav, wbv, wbg, wga, wgb,
                    ab_o, cv_o, u_o, sga_o, sgb_o, xn_s):
    @pl.when(pl.program_id(1) == 0)
    def _():
        xn_s[...] = _rms(x_ref[...], g_ref[...]).astype(BF16)

    xn = xn_s[...]
    ab_o[...] = _dot(xn, wab[...]).astype(BF16)
    cv_o[...] = (_dot(xn, wac[...]) * _dot(xn, wav[...])).astype(BF16)
    u_o[...] = (_dot(xn, wbv[...]) * jax.nn.sigmoid(_dot(xn, wbg[...]))).astype(BF16)
    sga_o[...] = jax.nn.sigmoid(_dot(xn, wga[...])).astype(BF16)
    sgb_o[...] = jax.nn.sigmoid(_dot(xn, wgb[...])).astype(BF16)


def _in_proj(x2d, g_mix, w_in, *, tm, tn):
    t, d = x2d.shape
    w = d // 2
    nj = w // tn
    grid = (t // tm, nj)

    def wspec(col0, width):
        blk0 = col0 // width
        return pl.BlockSpec((d, width), lambda i, j: (0, blk0 + j))

    in_specs = [
        pl.BlockSpec((tm, d), lambda i, j: (i, 0)),
        _resident((1, d), lambda i, j: (0, 0)),
        wspec(0, tn), wspec(w, tn), wspec(2 * w, tn), wspec(3 * w, tn), wspec(4 * w, tn),
        wspec(5 * w, 2 * tn), wspec(7 * w, 2 * tn),
    ]
    nspec = pl.BlockSpec((tm, tn), lambda i, j: (i, j))
    wide = pl.BlockSpec((tm, 2 * tn), lambda i, j: (i, j))
    out_shape = [jax.ShapeDtypeStruct((t, w), BF16)] * 3 + [jax.ShapeDtypeStruct((t, d), BF16)] * 2
    return pl.pallas_call(
        _in_proj_kernel,
        grid=grid,
        in_specs=in_specs,
        out_specs=[nspec, nspec, nspec, wide, wide],
        out_shape=out_shape,
        scratch_shapes=[pltpu.VMEM((tm, d), BF16)],
        compiler_params=_params(2),
        name="in_proj",
    )(x2d, g_mix, *([w_in] * 7))


def _mixer_kernel(x_ref, ab_ref, cv_ref, cvp_ref, cvn_ref, u_ref, up_ref, un_ref, sga_ref, sgb_ref,
                  caw_ref, woa_ref, cbw_ref, cbb_ref, lng_ref, lnb_ref, wob_ref, wo_ref,
                  o_ref, exta_s, extb_s, za_s, zb_s, *, rows):
    tm = x_ref.shape[0]
    i = pl.program_id(1)
    first = i == 0
    last = i == pl.num_programs(1) - 1

    def fill(ext, prev, cur, nxt):
        ext[0:HALO, :] = jnp.where(first, 0.0, prev[...].astype(F32))
        ext[HALO:HALO + tm, :] = cur[...].astype(F32)
        ext[HALO + tm:HALO + tm + HALO, :] = jnp.where(last, 0.0, nxt[...].astype(F32))

    fill(exta_s, cvp_ref, cv_ref, cvn_ref)
    fill(extb_s, up_ref, u_ref, un_ref)

    for c in range(tm // rows):
        r0 = c * rows
        ca = caw_ref[0:1, :] * exta_s[pl.ds(r0 + HALO - 1, rows), :]
        for k in range(1, K_A):
            ca = ca + caw_ref[k:k + 1, :] * exta_s[pl.ds(r0 + HALO - 1 + k, rows), :]
        za_s[pl.ds(r0, rows), :] = (ab_ref[pl.ds(r0, rows), :].astype(F32) * ca).astype(BF16)
        cb = cbb_ref[...] + cbw_ref[0:1, :] * extb_s[pl.ds(r0 + HALO - K_B // 2, rows), :]
        for k in range(1, K_B):
            cb = cb + cbw_ref[k:k + 1, :] * extb_s[pl.ds(r0 + HALO - K_B // 2 + k, rows), :]
        mu = jnp.mean(cb, axis=-1, keepdims=True)
        cen = cb - mu
        var = jnp.mean(cen * cen, axis=-1, keepdims=True)
        ln = cen * lax.rsqrt(var + EPS) * lng_ref[...] + lnb_ref[...]
        zb_s[pl.ds(r0, rows), :] = (ln * jax.nn.sigmoid(ln)).astype(BF16)

    ya = _dot(za_s[...], woa_ref[...])
    yb = _dot(zb_s[...], wob_ref[...])
    merged = (sga_ref[...].astype(F32) * ya + sgb_ref[...].astype(F32) * yb).astype(BF16)
    o_ref[...] = x_ref[...] + _dot(merged, wo_ref[...])


def _mixer(x, ab, cv, u, sga, sgb, conv_a_w, w_out_a, conv_b_w, conv_b_bias, ln_g, ln_b, w_out_b, w_o,
           *, tm, rows):
    b, s, d = x.shape
    w = d // 2
    hb = tm // HALO
    n_hb = s // HALO
    grid = (b, s // tm)

    def tile(c):
        return pl.BlockSpec((None, tm, c), lambda bi, i: (bi, i, 0))

    prev = pl.BlockSpec((None, HALO, w), lambda bi, i: (bi, jnp.maximum(i * hb - 1, 0), 0))
    nxt = pl.BlockSpec((None, HALO, w), lambda bi, i: (bi, jnp.minimum((i + 1) * hb, n_hb - 1), 0))

    def const(shape):
        return _resident(shape, lambda bi, i: (0,) * len(shape))

    in_specs = [
        tile(d), tile(w), tile(w), prev, nxt, tile(w), prev, nxt, tile(d), tile(d),
        const((K_A, w)), const((w, d)), const((K_B, w)), const((1, w)), const((1, w)), const((1, w)),
        const((w, d)), const((d, d)),
    ]
    return pl.pallas_call(
        functools.partial(_mixer_kernel, rows=rows),
        grid=grid,
        in_specs=in_specs,
        out_specs=tile(d),
        out_shape=jax.ShapeDtypeStruct((b, s, d), F32),
        scratch_shapes=[pltpu.VMEM((tm + 2 * HALO, w), F32), pltpu.VMEM((tm + 2 * HALO, w), F32),
                        pltpu.VMEM((tm, w), BF16), pltpu.VMEM((tm, w), BF16)],
        compiler_params=_params(2),
        name="mixer",
    )(x, ab, cv, cv, cv, u, u, u, sga, sgb,
      conv_a_w, w_out_a, conv_b_w, conv_b_bias, ln_g, ln_b, w_out_b, w_o)


def _kv_kernel(mem_ref, g_ref, wk_ref, wv_ref, kt_o, v_o, mn_s):
    @pl.when(pl.program_id(1) == 0)
    def _():
        mn_s[...] = _rms(mem_ref[...], g_ref[...]).astype(BF16)

    mn = mn_s[...]
    kt_o[...] = _dot(mn, wk_ref[...]).T.astype(BF16)
    v_o[...] = _dot(mn, wv_ref[...]).astype(BF16)


def _kv_proj(mem, g_mem, w_kv, *, tn):
    b, n_mem, d = mem.shape
    nj = d // tn
    return pl.pallas_call(
        _kv_kernel,
        grid=(b, nj),
        in_specs=[
            pl.BlockSpec((None, n_mem, d), lambda bi, j: (bi, 0, 0)),
            _resident((1, d), lambda bi, j: (0, 0)),
            pl.BlockSpec((d, tn), lambda bi, j: (0, j)),
            pl.BlockSpec((d, tn), lambda bi, j: (0, nj + j)),
        ],
        out_specs=[pl.BlockSpec((None, tn, n_mem), lambda bi, j: (bi, j, 0)),
                   pl.BlockSpec((None, n_mem, tn), lambda bi, j: (bi, 0, j))],
        out_shape=[jax.ShapeDtypeStruct((b, d, n_mem), BF16), jax.ShapeDtypeStruct((b, n_mem, d), BF16)],
        scratch_shapes=[pltpu.VMEM((n_mem, d), BF16)],
        compiler_params=_params(2),
        name="kv_proj",
    )(mem, g_mem, w_kv, w_kv)


def _xattn_kernel(x_ref, g_ref, wq_ref, kt_ref, v_ref, wxo_ref, o_ref, hn_s, oh_s):
    d = x_ref.shape[1]
    hd = d // N_XHEADS
    scale = hd ** -0.5
    hn_s[...] = _rms(x_ref[...], g_ref[...]).astype(BF16)
    for h in range(N_XHEADS):
        cols = slice(h * hd, (h + 1) * hd)
        q = _dot(hn_s[...], wq_ref[:, cols]).astype(BF16)
        s = _dot(q, kt_ref[cols, :]) * scale
        p = jnp.exp(s - jnp.max(s, axis=-1, keepdims=True))
        p = p / jnp.sum(p, axis=-1, keepdims=True)
        oh_s[:, cols] = _dot(p.astype(BF16), v_ref[:, cols]).astype(BF16)
    o_ref[...] = x_ref[...] + _dot(oh_s[...], wxo_ref[...])


def _xattn(x, g_xattn, w_q, kt, v, w_xo, *, tm):
    b, s, d = x.shape
    n_mem = v.shape[1]
    tile = pl.BlockSpec((None, tm, d), lambda bi, i: (bi, i, 0))
    return pl.pallas_call(
        _xattn_kernel,
        grid=(b, s // tm),
        in_specs=[
            tile,
            _resident((1, d), lambda bi, i: (0, 0)),
            _resident((d, d), lambda bi, i: (0, 0)),
            pl.BlockSpec((None, d, n_mem), lambda bi, i: (bi, 0, 0)),
            pl.BlockSpec((None, n_mem, d), lambda bi, i: (bi, 0, 0)),
            _resident((d, d), lambda bi, i: (0, 0)),
        ],
        out_specs=tile,
        out_shape=jax.ShapeDtypeStruct((b, s, d), F32),
        scratch_shapes=[pltpu.VMEM((tm, d), BF16), pltpu.VMEM((tm, d), BF16)],
        compiler_params=_params(2),
        name="xattn",
    )(x, g_xattn, w_q, kt, v, w_xo)


def _ffn_kernel(x_ref, g_ref, wg_ref, wu_ref, wd_ref, gf_ref, o_ref, hn_s, acc_s):
    j = pl.program_id(1)

    @pl.when(j == 0)
    def _():
        hn_s[...] = _rms(x_ref[...], g_ref[...]).astype(BF16)
        acc_s[...] = jnp.zeros_like(acc_s)

    hn = hn_s[...]
    gt = _dot(hn, wg_ref[...])
    up = _dot(hn, wu_ref[...])
    h = (gt * jax.nn.sigmoid(gt) * up).astype(BF16)
    acc_s[...] += _dot(h, wd_ref[...])

    @pl.when(j == pl.num_programs(1) - 1)
    def _():
        o_ref[...] = _rms(x_ref[...] + acc_s[...], gf_ref[...])


def _ffn(x2d, g_ffn, w_gate_up, w_down, g_final, *, tm, tf):
    t, d = x2d.shape
    d_ff = w_down.shape[0]
    nj = d_ff // tf
    tile = pl.BlockSpec((tm, d), lambda i, j: (i, 0))
    return pl.pallas_call(
        _ffn_kernel,
        grid=(t // tm, nj),
        in_specs=[
            tile,
            _resident((1, d), lambda i, j: (0, 0)),
            pl.BlockSpec((d, tf), lambda i, j: (0, j)),
            pl.BlockSpec((d, tf), lambda i, j: (0, nj + j)),
            pl.BlockSpec((tf, d), lambda i, j: (j, 0)),
            _resident((1, d), lambda i, j: (0, 0)),
        ],
        out_specs=tile,
        out_shape=jax.ShapeDtypeStruct((t, d), F32),
        scratch_shapes=[pltpu.VMEM((tm, d), BF16), pltpu.VMEM((tm, d), F32)],
        compiler_params=_params(2),
        name="ffn",
    )(x2d, g_ffn, w_gate_up, w_gate_up, w_down, g_final)


def _tile(n, want):
    for t in range(min(want, n), 15, -1):
        if n % t == 0 and t % 16 == 0:
            return t
    return n


def _trunk(x, mem, p):
    b, s, d = x.shape
    w = d // 2
    t = b * s
    row = lambda a: a.reshape(1, -1)

    ab, cv, u, sga, sgb = _in_proj(x.reshape(t, d), row(p["g_mix"]), p["w_in"],
                                   tm=_tile(t, 512), tn=_tile(w, 256))
    tm_mix = _tile(s, 256)
    x1 = _mixer(x, ab.reshape(b, s, w), cv.reshape(b, s, w), u.reshape(b, s, w),
                sga.reshape(b, s, d), sgb.reshape(b, s, d),
                p["conv_a_w"], p["w_out_a"], p["conv_b_w"], row(p["conv_b_bias"]),
                row(p["ln_b_g"]), row(p["ln_b_b"]), p["w_out_b"], p["w_o"],
                tm=tm_mix, rows=_tile(tm_mix, 32))
    kt, v = _kv_proj(mem, row(p["g_mem"]), p["w_kv"], tn=_tile(d, 512))
    x2 = _xattn(x1, row(p["g_xattn"]), p["w_q"], kt, v, p["w_xo"], tm=_tile(s, 512))
    d_ff = p["w_down"].shape[0]
    y = _ffn(x2.reshape(t, d), row(p["g_ffn"]), p["w_gate_up"], p["w_down"], row(p["g_final"]),
             tm=_tile(t, 512), tf=_tile(d_ff, 512))
    return y.reshape(b, s, d)


def kernel(x_prompt, x_sample, mem_prompt, mem_sample, g_mix, w_in, conv_a_w, w_out_a, conv_b_w,
           conv_b_bias, ln_b_g, ln_b_b, w_out_b, w_o, g_xattn, g_mem, w_q, w_kv, w_xo, g_ffn,
           w_gate_up, w_down, g_final):
    assert g_mix.shape[0] == 1, "single-layer trunk"
    p = dict(
        g_mix=g_mix[0], conv_a_w=conv_a_w[0], conv_b_w=conv_b_w[0], conv_b_bias=conv_b_bias[0],
        ln_b_g=ln_b_g[0], ln_b_b=ln_b_b[0], g_xattn=g_xattn[0], g_mem=g_mem[0], g_ffn=g_ffn[0],
        g_final=g_final,
        w_in=w_in[0].astype(BF16), w_out_a=w_out_a[0].astype(BF16), w_out_b=w_out_b[0].astype(BF16),
        w_o=w_o[0].astype(BF16), w_q=w_q[0].astype(BF16), w_kv=w_kv[0].astype(BF16),
        w_xo=w_xo[0].astype(BF16), w_gate_up=w_gate_up[0].astype(BF16), w_down=w_down[0].astype(BF16),
    )
    return (_trunk(x_prompt, mem_prompt, p), _trunk(x_sample, mem_sample, p))
```

```python
import functools

import jax
import jax.numpy as jnp
from jax import lax
from jax.experimental import pallas as pl
from jax.experimental.pallas import tpu as pltpu

EPS = 1e-6
N_XHEADS = 4
K_A = 3
K_B = 31
HALO = 16
LANES = 128
SUBLANES = 8
V7X_VMEM_LIMIT_BYTES = 56 * 1024 * 1024

F32 = jnp.float32
BF16 = jnp.bfloat16


def _params(n_grid):
    return pltpu.CompilerParams(
        dimension_semantics=("arbitrary",) * n_grid,
        vmem_limit_bytes=V7X_VMEM_LIMIT_BYTES)


def _resident(shape, index_map):
    return pl.BlockSpec(shape, index_map, pipeline_mode=pl.Buffered(1))


def _rms(x, g):
    return x * lax.rsqrt(jnp.mean(x * x, axis=-1, keepdims=True) + EPS) * g


def _dot(a, b):
    return jnp.dot(a, b, preferred_element_type=F32)


def _depthwise_conv(ext_s, w_s, out_s, init, n_taps, tm):
    n_slabs = ext_s.shape[0]
    span = 2 * SUBLANES
    off = HALO - n_taps // 2
    for s in range(n_slabs):
        for r0 in range(0, tm, span):
            acc = [init[s], init[s]]
            for k in range(n_taps):
                w = w_s[k, s]
                for ph in range(2):
                    acc[ph] = acc[ph] + w * ext_s[s, pl.ds(r0 + ph + off + k, SUBLANES, stride=2), :]
            for ph in range(2):
                out_s[s, pl.ds(r0 + ph, SUBLANES, stride=2), :] = acc[ph]


def _in_proj_kernel(x_ref, xp_ref, xn_ref, g_ref, wab, wac, wav, wbv, wbg, wga, wgb,
                    caw_ref, cbw_ref, cbb_ref,
                    za_o, cb_o, sga_o, sgb_o,
                    xe_s, exta_s, extb_s, ca_s, cbo_s, wa_s, wb_s):
    tm = x_ref.shape[0]
    n_slabs = exta_s.shape[0]
    i = pl.program_id(1)

    @pl.when(pl.program_id(2) == 0)
    def _():
        g = g_ref[...]
        inside_prev = i > 0
        inside_next = i < pl.num_programs(1) - 1
        xe_s[0:HALO, :] = jnp.where(inside_prev, _rms(xp_ref[...], g), 0.0).astype(BF16)
        xe_s[HALO:HALO + tm, :] = _rms(x_ref[...], g).astype(BF16)
        xe_s[HALO + tm:HALO + tm + HALO, :] = jnp.where(inside_next, _rms(xn_ref[...], g), 0.0).astype(BF16)

    lanes = [slice(s * LANES, (s + 1) * LANES) for s in range(n_slabs)]
    bias = []
    for s in range(n_slabs):
        for k in range(K_A):
            wa_s[k, s] = jnp.broadcast_to(caw_ref[k:k + 1, lanes[s]], (SUBLANES, LANES))
        for k in range(K_B):
            wb_s[k, s] = jnp.broadcast_to(cbw_ref[k:k + 1, lanes[s]], (SUBLANES, LANES))
        bias.append(jnp.broadcast_to(cbb_ref[0:1, lanes[s]], (SUBLANES, LANES)))
    zero = [jnp.zeros((SUBLANES, LANES), F32)] * n_slabs

    xe = xe_s[...]
    xc = xe_s[HALO:HALO + tm, :]
    u = _dot(xe, wbv[...]) * jax.nn.sigmoid(_dot(xe, wbg[...]))
    cv = _dot(xe, wac[...]) * _dot(xe, wav[...])
    for s in range(n_slabs):
        extb_s[s] = u[:, lanes[s]]
        exta_s[s] = cv[:, lanes[s]]
    ab = _dot(xc, wab[...])
    sga_o[...] = jax.nn.sigmoid(_dot(xc, wga[...])).astype(BF16)
    sgb_o[...] = jax.nn.sigmoid(_dot(xc, wgb[...])).astype(BF16)

    _depthwise_conv(extb_s, wb_s, cbo_s, bias, K_B, tm)
    _depthwise_conv(exta_s, wa_s, ca_s, zero, K_A, tm)
    for s in range(n_slabs):
        cb_o[:, lanes[s]] = cbo_s[s]
        za_o[:, lanes[s]] = (ab[:, lanes[s]] * ca_s[s]).astype(BF16)


def _in_proj(x, g_mix, w_in, conv_a_w, conv_b_w, conv_b_bias, *, tm, tn):
    b, s, d = x.shape
    w = d // 2
    n_slabs = tn // LANES
    hb = tm // HALO
    n_hb = s // HALO
    grid = (b, s // tm, w // tn)

    def wspec(col0, width):
        blk0 = col0 // width
        return pl.BlockSpec((d, width), lambda bi, i, j: (0, blk0 + j))

    def cspec(rows):
        return pl.BlockSpec((rows, tn), lambda bi, i, j: (0, j))

    in_specs = [
        pl.BlockSpec((None, tm, d), lambda bi, i, j: (bi, i, 0)),
        pl.BlockSpec((None, HALO, d), lambda bi, i, j: (bi, jnp.maximum(i * hb - 1, 0), 0)),
        pl.BlockSpec((None, HALO, d), lambda bi, i, j: (bi, jnp.minimum((i + 1) * hb, n_hb - 1), 0)),
        _resident((1, d), lambda bi, i, j: (0, 0)),
        wspec(0, tn), wspec(w, tn), wspec(2 * w, tn), wspec(3 * w, tn), wspec(4 * w, tn),
        wspec(5 * w, 2 * tn), wspec(7 * w, 2 * tn),
        cspec(K_A), cspec(K_B), cspec(1),
    ]
    nspec = pl.BlockSpec((None, tm, tn), lambda bi, i, j: (bi, i, j))
    wide = pl.BlockSpec((None, tm, 2 * tn), lambda bi, i, j: (bi, i, j))
    out_shape = [jax.ShapeDtypeStruct((b, s, w), BF16), jax.ShapeDtypeStruct((b, s, w), F32),
                 jax.ShapeDtypeStruct((b, s, d), BF16), jax.ShapeDtypeStruct((b, s, d), BF16)]
    ext = pltpu.VMEM((n_slabs, tm + 2 * HALO, LANES), F32)
    out = pltpu.VMEM((n_slabs, tm, LANES), F32)
    return pl.pallas_call(
        _in_proj_kernel,
        grid=grid,
        in_specs=in_specs,
        out_specs=[nspec, nspec, wide, wide],
        out_shape=out_shape,
        scratch_shapes=[pltpu.VMEM((tm + 2 * HALO, d), BF16), ext, ext, out, out,
                        pltpu.VMEM((K_A, n_slabs, SUBLANES, LANES), F32),
                        pltpu.VMEM((K_B, n_slabs, SUBLANES, LANES), F32)],
        compiler_params=_params(3),
        name="in_proj",
    )(x, x, x, g_mix, *([w_in] * 7), conv_a_w, conv_b_w, conv_b_bias)


def _mixer_kernel(x_ref, za_ref, cb_ref, sga_ref, sgb_ref, lng_ref, lnb_ref,
                  woa_ref, wob_ref, wo_ref, o_ref):
    cb = cb_ref[...]
    cen = cb - jnp.mean(cb, axis=-1, keepdims=True)
    var = jnp.mean(cen * cen, axis=-1, keepdims=True)
    ln = cen * lax.rsqrt(var + EPS) * lng_ref[...] + lnb_ref[...]
    zb = (ln * jax.nn.sigmoid(ln)).astype(BF16)
    ya = _dot(za_ref[...], woa_ref[...])
    yb = _dot(zb, wob_ref[...])
    merged = (sga_ref[...].astype(F32) * ya + sgb_ref[...].astype(F32) * yb).astype(BF16)
    o_ref[...] = x_ref[...] + _dot(merged, wo_ref[...])


def _mixer(x, za, cb, sga, sgb, ln_g, ln_b, w_out_a, w_out_b, w_o, *, tm):
    b, s, d = x.shape
    w = d // 2

    def tile(c):
        return pl.BlockSpec((None, tm, c), lambda bi, i: (bi, i, 0))

    def const(shape):
        return _resident(shape, lambda bi, i: (0,) * len(shape))

    return pl.pallas_call(
        _mixer_kernel,
        grid=(b, s // tm),
        in_specs=[tile(d), tile(w), tile(w), tile(d), tile(d), const((1, w)), const((1, w)),
                  const((w, d)), const((w, d)), const((d, d))],
        out_specs=tile(d),
        out_shape=jax.ShapeDtypeStruct((b, s, d), F32),
        compiler_params=_params(2),
        name="mixer",
    )(x, za, cb, sga, sgb, ln_g, ln_b, w_out_a, w_out_b, w_o)


def _kv_kernel(mem_ref, g_ref, wk_ref, wv_ref, kt_o, v_o, mn_s):
    @pl.when(pl.program_id(1) == 0)
    def _():
        mn_s[...] = _rms(mem_ref[...], g_ref[...]).astype(BF16)

    mn = mn_s[...]
    kt_o[...] = _dot(mn, wk_ref[...]).T.astype(BF16)
    v_o[...] = _dot(mn, wv_ref[...]).astype(BF16)


def _kv_proj(mem, g_mem, w_kv, *, tn):
    b, n_mem, d = mem.shape
    nj = d // tn
    return pl.pallas_call(
        _kv_kernel,
        grid=(b, nj),
        in_specs=[
            pl.BlockSpec((None, n_mem, d), lambda bi, j: (bi, 0, 0)),
            _resident((1, d), lambda bi, j: (0, 0)),
            pl.BlockSpec((d, tn), lambda bi, j: (0, j)),
            pl.BlockSpec((d, tn), lambda bi, j: (0, nj + j)),
        ],
        out_specs=[pl.BlockSpec((None, tn, n_mem), lambda bi, j: (bi, j, 0)),
                   pl.BlockSpec((None, n_mem, tn), lambda bi, j: (bi, 0, j))],
        out_shape=[jax.ShapeDtypeStruct((b, d, n_mem), BF16), jax.ShapeDtypeStruct((b, n_mem, d), BF16)],
        scratch_shapes=[pltpu.VMEM((n_mem, d), BF16)],
        compiler_params=_params(2),
        name="kv_proj",
    )(mem, g_mem, w_kv, w_kv)


def _xattn_kernel(x_ref, g_ref, wq_ref, kt_ref, v_ref, wxo_ref, o_ref, hn_s, oh_s):
    d = x_ref.shape[1]
    hd = d // N_XHEADS
    scale = hd ** -0.5
    hn_s[...] = _rms(x_ref[...], g_ref[...]).astype(BF16)
    for h in range(N_XHEADS):
        cols = slice(h * hd, (h + 1) * hd)
        q = _dot(hn_s[...], wq_ref[:, cols]).astype(BF16)
        s = _dot(q, kt_ref[cols, :]) * scale
        p = jnp.exp(s - jnp.max(s, axis=-1, keepdims=True))
        p = p / jnp.sum(p, axis=-1, keepdims=True)
        oh_s[:, cols] = _dot(p.astype(BF16), v_ref[:, cols]).astype(BF16)
    o_ref[...] = x_ref[...] + _dot(oh_s[...], wxo_ref[...])


def _xattn(x, g_xattn, w_q, kt, v, w_xo, *, tm):
    b, s, d = x.shape
    n_mem = v.shape[1]
    tile = pl.BlockSpec((None, tm, d), lambda bi, i: (bi, i, 0))
    return pl.pallas_call(
        _xattn_kernel,
        grid=(b, s // tm),
        in_specs=[
            tile,
            _resident((1, d), lambda bi, i: (0, 0)),
            _resident((d, d), lambda bi, i: (0, 0)),
            pl.BlockSpec((None, d, n_mem), lambda bi, i: (bi, 0, 0)),
            pl.BlockSpec((None, n_mem, d), lambda bi, i: (bi, 0, 0)),
            _resident((d, d), lambda bi, i: (0, 0)),
        ],
        out_specs=tile,
        out_shape=jax.ShapeDtypeStruct((b, s, d), F32),
        scratch_shapes=[pltpu.VMEM((tm, d), BF16), pltpu.VMEM((tm, d), BF16)],
        compiler_params=_params(2),
        name="xattn",
    )(x, g_xattn, w_q, kt, v, w_xo)


def _ffn_kernel(x_ref, g_ref, wg_ref, wu_ref, wd_ref, gf_ref, o_ref, hn_s, acc_s):
    j = pl.program_id(1)

    @pl.when(j == 0)
    def _():
        hn_s[...] = _rms(x_ref[...], g_ref[...]).astype(BF16)
        acc_s[...] = jnp.zeros_like(acc_s)

    hn = hn_s[...]
    gt = _dot(hn, wg_ref[...])
    up = _dot(hn, wu_ref[...])
    h = (gt * jax.nn.sigmoid(gt) * up).astype(BF16)
    acc_s[...] += _dot(h, wd_ref[...])

    @pl.when(j == pl.num_programs(1) - 1)
    def _():
        o_ref[...] = _rms(x_ref[...] + acc_s[...], gf_ref[...])


def _ffn(x2d, g_ffn, w_gate_up, w_down, g_final, *, tm, tf):
    t, d = x2d.shape
    d_ff = w_down.shape[0]
    nj = d_ff // tf
    tile = pl.BlockSpec((tm, d), lambda i, j: (i, 0))
    return pl.pallas_call(
        _ffn_kernel,
        grid=(t // tm, nj),
        in_specs=[
            tile,
            _resident((1, d), lambda i, j: (0, 0)),
            pl.BlockSpec((d, tf), lambda i, j: (0, j)),
            pl.BlockSpec((d, tf), lambda i, j: (0, nj + j)),
            pl.BlockSpec((tf, d), lambda i, j: (j, 0)),
            _resident((1, d), lambda i, j: (0, 0)),
        ],
        out_specs=tile,
        out_shape=jax.ShapeDtypeStruct((t, d), F32),
        scratch_shapes=[pltpu.VMEM((tm, d), BF16), pltpu.VMEM((tm, d), F32)],
        compiler_params=_params(2),
        name="ffn",
    )(x2d, g_ffn, w_gate_up, w_gate_up, w_down, g_final)


def _tile(n, want):
    for t in range(min(want, n), 15, -1):
        if n % t == 0 and t % 16 == 0:
            return t
    return n


def _trunk(x, mem, p):
    b, s, d = x.shape
    w = d // 2
    t = b * s
    row = lambda a: a.reshape(1, -1)

    za, cb, sga, sgb = _in_proj(x, row(p["g_mix"]), p["w_in"], p["conv_a_w"], p["conv_b_w"],
                                row(p["conv_b_bias"]), tm=_tile(s, 512), tn=_tile(w, 256))
    x1 = _mixer(x, za, cb, sga, sgb, row(p["ln_b_g"]), row(p["ln_b_b"]),
                p["w_out_a"], p["w_out_b"], p["w_o"], tm=_tile(s, 256))
    kt, v = _kv_proj(mem, row(p["g_mem"]), p["w_kv"], tn=_tile(d, 512))
    x2 = _xattn(x1, row(p["g_xattn"]), p["w_q"], kt, v, p["w_xo"], tm=_tile(s, 512))
    d_ff = p["w_down"].shape[0]
    y = _ffn(x2.reshape(t, d), row(p["g_ffn"]), p["w_gate_up"], p["w_down"], row(p["g_final"]),
             tm=_tile(t, 512), tf=_tile(d_ff, 512))
    return y.reshape(b, s, d)


def kernel(x_prompt, x_sample, mem_prompt, mem_sample, g_mix, w_in, conv_a_w, w_out_a, conv_b_w,
           conv_b_bias, ln_b_g, ln_b_b, w_out_b, w_o, g_xattn, g_mem, w_q, w_kv, w_xo, g_ffn,
           w_gate_up, w_down, g_final):
    assert g_mix.shape[0] == 1, "single-layer trunk"
    p = dict(
        g_mix=g_mix[0], conv_a_w=conv_a_w[0], conv_b_w=conv_b_w[0], conv_b_bias=conv_b_bias[0],
        ln_b_g=ln_b_g[0], ln_b_b=ln_b_b[0], g_xattn=g_xattn[0], g_mem=g_mem[0], g_ffn=g_ffn[0],
        g_final=g_final,
        w_in=w_in[0].astype(BF16), w_out_a=w_out_a[0].astype(BF16), w_out_b=w_out_b[0].astype(BF16),
        w_o=w_o[0].astype(BF16), w_q=w_q[0].astype(BF16), w_kv=w_kv[0].astype(BF16),
        w_xo=w_xo[0].astype(BF16), w_gate_up=w_gate_up[0].astype(BF16), w_down=w_down[0].astype(BF16),
    )
    return (_trunk(x_prompt, mem_prompt, p), _trunk(x_sample, mem_sample, p))
```

```python
import functools

import jax
import jax.numpy as jnp
from jax import lax
from jax.experimental import pallas as pl
from jax.experimental.pallas import tpu as pltpu

EPS = 1e-6
N_XHEADS = 4
K_A = 3
K_B = 31
HALO = 16
LANES = 128
SUBLANES = 8
V7X_VMEM_LIMIT_BYTES = 60 * 1024 * 1024

F32 = jnp.float32
BF16 = jnp.bfloat16


def _params(n_grid):
    return pltpu.CompilerParams(
        dimension_semantics=("arbitrary",) * n_grid,
        vmem_limit_bytes=V7X_VMEM_LIMIT_BYTES)


def _resident(shape, index_map):
    return pl.BlockSpec(shape, index_map, pipeline_mode=pl.Buffered(1))


def _rms(x, g):
    return x * lax.rsqrt(jnp.mean(x * x, axis=-1, keepdims=True) + EPS) * g


def _dot(a, b):
    return jnp.dot(a, b, preferred_element_type=F32)


def _depthwise_conv(ext_s, w_s, out_s, init, n_taps, tm):
    n_slabs = ext_s.shape[0]
    span = 2 * SUBLANES
    off = HALO - n_taps // 2
    for s in range(n_slabs):
        for r0 in range(0, tm, span):
            acc = [init[s], init[s]]
            for k in range(n_taps):
                w = w_s[k, s]
                for ph in range(2):
                    acc[ph] = acc[ph] + w * ext_s[s, pl.ds(r0 + ph + off + k, SUBLANES, stride=2), :]
            for ph in range(2):
                out_s[s, pl.ds(r0 + ph, SUBLANES, stride=2), :] = acc[ph]


def _in_proj_kernel(x_ref, xp_ref, xn_ref, g_ref, wab, wac, wav, wbv, wbg, wga, wgb,
                    caw_ref, cbw_ref, cbb_ref,
                    za_o, cb_o, sga_o, sgb_o,
                    xe_s, exta_s, extb_s, ca_s, cbo_s, wa_s, wb_s):
    tm = x_ref.shape[0]
    n_slabs = exta_s.shape[0]
    i = pl.program_id(1)

    @pl.when(pl.program_id(2) == 0)
    def _():
        g = g_ref[...]
        inside_prev = i > 0
        inside_next = i < pl.num_programs(1) - 1
        xe_s[0:HALO, :] = jnp.where(inside_prev, _rms(xp_ref[...], g), 0.0).astype(BF16)
        xe_s[HALO:HALO + tm, :] = _rms(x_ref[...], g).astype(BF16)
        xe_s[HALO + tm:HALO + tm + HALO, :] = jnp.where(inside_next, _rms(xn_ref[...], g), 0.0).astype(BF16)

    lanes = [slice(s * LANES, (s + 1) * LANES) for s in range(n_slabs)]
    bias = []
    for s in range(n_slabs):
        for k in range(K_A):
            wa_s[k, s] = jnp.broadcast_to(caw_ref[k:k + 1, lanes[s]], (SUBLANES, LANES))
        for k in range(K_B):
            wb_s[k, s] = jnp.broadcast_to(cbw_ref[k:k + 1, lanes[s]], (SUBLANES, LANES))
        bias.append(jnp.broadcast_to(cbb_ref[0:1, lanes[s]], (SUBLANES, LANES)))
    zero = [jnp.zeros((SUBLANES, LANES), F32)] * n_slabs

    xe = xe_s[...]
    xc = xe_s[HALO:HALO + tm, :]
    u = _dot(xe, wbv[...]) * jax.nn.sigmoid(_dot(xe, wbg[...]))
    cv = _dot(xe, wac[...]) * _dot(xe, wav[...])
    for s in range(n_slabs):
        extb_s[s] = u[:, lanes[s]]
        exta_s[s] = cv[:, lanes[s]]
    ab = _dot(xc, wab[...])
    sga_o[...] = jax.nn.sigmoid(_dot(xc, wga[...])).astype(BF16)
    sgb_o[...] = jax.nn.sigmoid(_dot(xc, wgb[...])).astype(BF16)

    _depthwise_conv(extb_s, wb_s, cbo_s, bias, K_B, tm)
    _depthwise_conv(exta_s, wa_s, ca_s, zero, K_A, tm)
    for s in range(n_slabs):
        cb_o[:, lanes[s]] = cbo_s[s]
        za_o[:, lanes[s]] = (ab[:, lanes[s]] * ca_s[s]).astype(BF16)


def _in_proj(x, g_mix, w_in, conv_a_w, conv_b_w, conv_b_bias, *, tm, tn):
    b, s, d = x.shape
    w = d // 2
    n_slabs = tn // LANES
    hb = tm // HALO
    n_hb = s // HALO
    grid = (b, s // tm, w // tn)

    def wspec(col0, width):
        blk0 = col0 // width
        return pl.BlockSpec((d, width), lambda bi, i, j: (0, blk0 + j))

    def cspec(rows):
        return pl.BlockSpec((rows, tn), lambda bi, i, j: (0, j))

    in_specs = [
        pl.BlockSpec((None, tm, d), lambda bi, i, j: (bi, i, 0)),
        pl.BlockSpec((None, HALO, d), lambda bi, i, j: (bi, jnp.maximum(i * hb - 1, 0), 0)),
        pl.BlockSpec((None, HALO, d), lambda bi, i, j: (bi, jnp.minimum((i + 1) * hb, n_hb - 1), 0)),
        _resident((1, d), lambda bi, i, j: (0, 0)),
        wspec(0, tn), wspec(w, tn), wspec(2 * w, tn), wspec(3 * w, tn), wspec(4 * w, tn),
        wspec(5 * w, 2 * tn), wspec(7 * w, 2 * tn),
        cspec(K_A), cspec(K_B), cspec(1),
    ]
    nspec = pl.BlockSpec((None, tm, tn), lambda bi, i, j: (bi, i, j))
    wide = pl.BlockSpec((None, tm, 2 * tn), lambda bi, i, j: (bi, i, j))
    out_shape = [jax.ShapeDtypeStruct((b, s, w), BF16), jax.ShapeDtypeStruct((b, s, w), F32),
                 jax.ShapeDtypeStruct((b, s, d), BF16), jax.ShapeDtypeStruct((b, s, d), BF16)]
    ext = pltpu.VMEM((n_slabs, tm + 2 * HALO, LANES), F32)
    out = pltpu.VMEM((n_slabs, tm, LANES), F32)
    return pl.pallas_call(
        _in_proj_kernel,
        grid=grid,
        in_specs=in_specs,
        out_specs=[nspec, nspec, wide, wide],
        out_shape=out_shape,
        scratch_shapes=[pltpu.VMEM((tm + 2 * HALO, d), BF16), ext, ext, out, out,
                        pltpu.VMEM((K_A, n_slabs, SUBLANES, LANES), F32),
                        pltpu.VMEM((K_B, n_slabs, SUBLANES, LANES), F32)],
        compiler_params=_params(3),
        name="in_proj",
    )(x, x, x, g_mix, *([w_in] * 7), conv_a_w, conv_b_w, conv_b_bias)


def _mixer_kernel(x_ref, za_ref, cb_ref, sga_ref, sgb_ref, lng_ref, lnb_ref,
                  woa_ref, wob_ref, wo_ref, o_ref):
    cb = cb_ref[...]
    cen = cb - jnp.mean(cb, axis=-1, keepdims=True)
    var = jnp.mean(cen * cen, axis=-1, keepdims=True)
    ln = cen * lax.rsqrt(var + EPS) * lng_ref[...] + lnb_ref[...]
    zb = (ln * jax.nn.sigmoid(ln)).astype(BF16)
    ya = _dot(za_ref[...], woa_ref[...])
    yb = _dot(zb, wob_ref[...])
    merged = (sga_ref[...].astype(F32) * ya + sgb_ref[...].astype(F32) * yb).astype(BF16)
    o_ref[...] = x_ref[...] + _dot(merged, wo_ref[...])


def _mixer(x, za, cb, sga, sgb, ln_g, ln_b, w_out_a, w_out_b, w_o, *, tm):
    b, s, d = x.shape
    w = d // 2

    def tile(c):
        return pl.BlockSpec((None, tm, c), lambda bi, i: (bi, i, 0))

    def const(shape):
        return _resident(shape, lambda bi, i: (0,) * len(shape))

    return pl.pallas_call(
        _mixer_kernel,
        grid=(b, s // tm),
        in_specs=[tile(d), tile(w), tile(w), tile(d), tile(d), const((1, w)), const((1, w)),
                  const((w, d)), const((w, d)), const((d, d))],
        out_specs=tile(d),
        out_shape=jax.ShapeDtypeStruct((b, s, d), F32),
        compiler_params=_params(2),
        name="mixer",
    )(x, za, cb, sga, sgb, ln_g, ln_b, w_out_a, w_out_b, w_o)


def _kv_kernel(mem_ref, g_ref, wk_ref, wv_ref, kt_o, v_o, mn_s):
    @pl.when(pl.program_id(0) == 0)
    def _():
        mn_s[...] = _rms(mem_ref[...], g_ref[...]).astype(BF16)

    n_mem = kt_o.shape[2]
    mn = mn_s[...]
    k = _dot(mn, wk_ref[...])
    for bi in range(kt_o.shape[0]):
        kt_o[bi] = k[bi * n_mem:(bi + 1) * n_mem, :].T.astype(BF16)
    v_o[...] = _dot(mn, wv_ref[...]).astype(BF16)


def _kv_proj(mem, g_mem, w_kv, *, tn):
    b, n_mem, d = mem.shape
    nj = d // tn
    rows = b * n_mem
    kt, v = pl.pallas_call(
        _kv_kernel,
        grid=(nj,),
        in_specs=[
            _resident((rows, d), lambda j: (0, 0)),
            _resident((1, d), lambda j: (0, 0)),
            pl.BlockSpec((d, tn), lambda j: (0, j)),
            pl.BlockSpec((d, tn), lambda j: (0, nj + j)),
        ],
        out_specs=[pl.BlockSpec((b, tn, n_mem), lambda j: (0, j, 0)),
                   pl.BlockSpec((rows, tn), lambda j: (0, j))],
        out_shape=[jax.ShapeDtypeStruct((b, d, n_mem), BF16), jax.ShapeDtypeStruct((rows, d), BF16)],
        scratch_shapes=[pltpu.VMEM((rows, d), BF16)],
        compiler_params=_params(1),
        name="kv_proj",
    )(mem.reshape(rows, d), g_mem, w_kv, w_kv)
    return kt, v.reshape(b, n_mem, d)


def _xattn_kernel(x_ref, g_ref, gn_ref, wq_ref, kt_ref, v_ref, wxo_ref, o_ref, hn_o, hn_s):
    d = x_ref.shape[1]
    hd = d // N_XHEADS
    scale = hd ** -0.5
    cols = [slice(h * hd, (h + 1) * hd) for h in range(N_XHEADS)]
    hn_s[...] = _rms(x_ref[...], g_ref[...]).astype(BF16)

    def query(h):
        return _dot(hn_s[...], wq_ref[:, cols[h]]).astype(BF16)

    def probs(q, h):
        s = _dot(q, kt_ref[cols[h], :]) * scale
        p = jnp.exp(s - jnp.max(s, axis=-1, keepdims=True))
        return (p / jnp.sum(p, axis=-1, keepdims=True)).astype(BF16)

    def attend(p, h):
        return _dot(p, v_ref[:, cols[h]]).astype(BF16)

    acc = x_ref[...]
    q = query(0)
    for h in range(N_XHEADS):
        p = probs(q, h)
        if h + 1 < N_XHEADS:
            q = query(h + 1)
        acc = acc + _dot(attend(p, h), wxo_ref[cols[h], :])
    o_ref[...] = acc
    hn_o[...] = _rms(acc, gn_ref[...]).astype(BF16)


def _xattn(x, g_xattn, g_next, w_q, kt, v, w_xo, *, tm):
    b, s, d = x.shape
    n_mem = v.shape[1]
    tile = pl.BlockSpec((None, tm, d), lambda bi, i: (bi, i, 0))
    return pl.pallas_call(
        _xattn_kernel,
        grid=(b, s // tm),
        in_specs=[
            tile,
            _resident((1, d), lambda bi, i: (0, 0)),
            _resident((1, d), lambda bi, i: (0, 0)),
            _resident((d, d), lambda bi, i: (0, 0)),
            pl.BlockSpec((None, d, n_mem), lambda bi, i: (bi, 0, 0)),
            pl.BlockSpec((None, n_mem, d), lambda bi, i: (bi, 0, 0)),
            _resident((d, d), lambda bi, i: (0, 0)),
        ],
        out_specs=[tile, tile],
        out_shape=[jax.ShapeDtypeStruct((b, s, d), F32), jax.ShapeDtypeStruct((b, s, d), BF16)],
        scratch_shapes=[pltpu.VMEM((tm, d), BF16)],
        compiler_params=_params(2),
        name="xattn",
    )(x, g_xattn, g_next, w_q, kt, v, w_xo)


def _ffn_kernel(hn_ref, x_hbm, wg_ref, wu_ref, wd_ref, gf_ref, o_ref, x_s, sem):
    tm = hn_ref.shape[0]
    i = pl.program_id(0)
    j = pl.program_id(1)
    fetch_x = pltpu.make_async_copy(x_hbm.at[pl.ds(i * tm, tm), :], x_s, sem)

    @pl.when(j == 0)
    def _():
        fetch_x.start()
        o_ref[...] = jnp.zeros_like(o_ref)

    hn = hn_ref[...]
    gt = _dot(hn, wg_ref[...])
    up = _dot(hn, wu_ref[...])
    o_ref[...] += _dot((gt * jax.nn.sigmoid(gt) * up).astype(BF16), wd_ref[...])

    @pl.when(j == pl.num_programs(1) - 1)
    def _():
        fetch_x.wait()
        o_ref[...] = _rms(x_s[...] + o_ref[...], gf_ref[...])


def _ffn(hn2d, x2d, w_gate_up, w_down, g_final, *, tm, tf):
    t, d = x2d.shape
    d_ff = w_down.shape[0]
    nj = d_ff // tf
    tile = pl.BlockSpec((tm, d), lambda i, j: (i, 0))
    return pl.pallas_call(
        _ffn_kernel,
        grid=(t // tm, nj),
        in_specs=[
            tile,
            pl.BlockSpec(memory_space=pl.ANY),
            pl.BlockSpec((d, tf), lambda i, j: (0, j)),
            pl.BlockSpec((d, tf), lambda i, j: (0, nj + j)),
            pl.BlockSpec((tf, d), lambda i, j: (j, 0)),
            _resident((1, d), lambda i, j: (0, 0)),
        ],
        out_specs=tile,
        out_shape=jax.ShapeDtypeStruct((t, d), F32),
        scratch_shapes=[pltpu.VMEM((tm, d), F32), pltpu.SemaphoreType.DMA(())],
        compiler_params=_params(2),
        name="ffn",
    )(hn2d, x2d, w_gate_up, w_gate_up, w_down, g_final)


def _tile(n, want):
    for t in range(min(want, n), 15, -1):
        if n % t == 0 and t % 16 == 0:
            return t
    return n


def _trunk(x, mem, p):
    b, s, d = x.shape
    w = d // 2
    t = b * s
    row = lambda a: a.reshape(1, -1)

    za, cb, sga, sgb = _in_proj(x, row(p["g_mix"]), p["w_in"], p["conv_a_w"], p["conv_b_w"],
                                row(p["conv_b_bias"]), tm=_tile(s, 1024), tn=_tile(w, 256))
    x1 = _mixer(x, za, cb, sga, sgb, row(p["ln_b_g"]), row(p["ln_b_b"]),
                p["w_out_a"], p["w_out_b"], p["w_o"], tm=_tile(s, 512))
    kt, v = _kv_proj(mem, row(p["g_mem"]), p["w_kv"], tn=_tile(d, 512))
    x2, hn = _xattn(x1, row(p["g_xattn"]), row(p["g_ffn"]), p["w_q"], kt, v, p["w_xo"], tm=_tile(s, 512))
    d_ff = p["w_down"].shape[0]
    y = _ffn(hn.reshape(t, d), x2.reshape(t, d), p["w_gate_up"], p["w_down"], row(p["g_final"]),
             tm=_tile(t, 1024), tf=_tile(d_ff, 512))
    return y.reshape(b, s, d)


def kernel(x_prompt, x_sample, mem_prompt, mem_sample, g_mix, w_in, conv_a_w, w_out_a, conv_b_w,
           conv_b_bias, ln_b_g, ln_b_b, w_out_b, w_o, g_xattn, g_mem, w_q, w_kv, w_xo, g_ffn,
           w_gate_up, w_down, g_final):
    assert g_mix.shape[0] == 1, "single-layer trunk"
    p = dict(
        g_mix=g_mix[0], conv_a_w=conv_a_w[0], conv_b_w=conv_b_w[0], conv_b_bias=conv_b_bias[0],
        ln_b_g=ln_b_g[0], ln_b_b=ln_b_b[0], g_xattn=g_xattn[0], g_mem=g_mem[0], g_ffn=g_ffn[0],
        g_final=g_final,
        w_in=w_in[0].astype(BF16), w_out_a=w_out_a[0].astype(BF16), w_out_b=w_out_b[0].astype(BF16),
        w_o=w_o[0].astype(BF16), w_q=w_q[0].astype(BF16), w_kv=w_kv[0].astype(BF16),
        w_xo=w_xo[0].astype(BF16), w_gate_up=w_gate_up[0].astype(BF16), w_down=w_down[0].astype(BF16),
    )
    return (_trunk(x_prompt, mem_prompt, p), _trunk(x_sample, mem_sample, p))
```

```python
import functools

import jax
import jax.numpy as jnp
from jax import lax
from jax.experimental import pallas as pl
from jax.experimental.pallas import tpu as pltpu

EPS = 1e-6
N_XHEADS = 4
K_A = 3
K_B = 31
HALO = 16
LANES = 128
SUBLANES = 8
V7X_VMEM_LIMIT_BYTES = 60 * 1024 * 1024

F32 = jnp.float32
BF16 = jnp.bfloat16


def _params(n_grid):
    return pltpu.CompilerParams(
        dimension_semantics=("arbitrary",) * n_grid,
        vmem_limit_bytes=V7X_VMEM_LIMIT_BYTES)


def _resident(shape, index_map):
    return pl.BlockSpec(shape, index_map, pipeline_mode=pl.Buffered(1))


def _rms(x, g):
    return x * lax.rsqrt(jnp.mean(x * x, axis=-1, keepdims=True) + EPS) * g


def _dot(a, b):
    return jnp.dot(a, b, preferred_element_type=F32)


def _depthwise_conv(ext_s, w_s, out_s, init, n_taps, tm):
    n_slabs = ext_s.shape[0]
    span = 2 * SUBLANES
    off = HALO - n_taps // 2
    for s in range(n_slabs):
        for r0 in range(0, tm, span):
            acc = [init[s], init[s]]
            for k in range(n_taps):
                w = w_s[k, s]
                for ph in range(2):
                    acc[ph] = acc[ph] + w * ext_s[s, pl.ds(r0 + ph + off + k, SUBLANES, stride=2), :]
            for ph in range(2):
                out_s[s, pl.ds(r0 + ph, SUBLANES, stride=2), :] = acc[ph]


def _in_proj_kernel(x_ref, xp_ref, xn_ref, g_ref, wab, wac, wav, wbv, wbg, wga, wgb,
                    caw_ref, cbw_ref, cbb_ref,
                    za_o, cb_o, sga_o, sgb_o,
                    xe_s, exta_s, extb_s, ca_s, cbo_s, wa_s, wb_s):
    tm = x_ref.shape[0]
    n_slabs = exta_s.shape[0]
    i = pl.program_id(1)

    @pl.when(pl.program_id(2) == 0)
    def _():
        g = g_ref[...]
        inside_prev = i > 0
        inside_next = i < pl.num_programs(1) - 1
        xe_s[0:HALO, :] = jnp.where(inside_prev, _rms(xp_ref[...], g), 0.0).astype(BF16)
        xe_s[HALO:HALO + tm, :] = _rms(x_ref[...], g).astype(BF16)
        xe_s[HALO + tm:HALO + tm + HALO, :] = jnp.where(inside_next, _rms(xn_ref[...], g), 0.0).astype(BF16)

    lanes = [slice(s * LANES, (s + 1) * LANES) for s in range(n_slabs)]
    bias = []
    for s in range(n_slabs):
        for k in range(K_A):
            wa_s[k, s] = jnp.broadcast_to(caw_ref[k:k + 1, lanes[s]], (SUBLANES, LANES))
        for k in range(K_B):
            wb_s[k, s] = jnp.broadcast_to(cbw_ref[k:k + 1, lanes[s]], (SUBLANES, LANES))
        bias.append(jnp.broadcast_to(cbb_ref[0:1, lanes[s]], (SUBLANES, LANES)))
    zero = [jnp.zeros((SUBLANES, LANES), F32)] * n_slabs

    xe = xe_s[...]
    xc = xe_s[HALO:HALO + tm, :]
    u = _dot(xe, wbv[...]) * jax.nn.sigmoid(_dot(xe, wbg[...]))
    cv = _dot(xe, wac[...]) * _dot(xe, wav[...])
    for s in range(n_slabs):
        extb_s[s] = u[:, lanes[s]]
        exta_s[s] = cv[:, lanes[s]]
    sga_o[...] = jax.nn.sigmoid(_dot(xc, wga[...])).astype(BF16)
    sgb_o[...] = jax.nn.sigmoid(_dot(xc, wgb[...])).astype(BF16)
    ab = _dot(xc, wab[...])

    _depthwise_conv(extb_s, wb_s, cbo_s, bias, K_B, tm)
    _depthwise_conv(exta_s, wa_s, ca_s, zero, K_A, tm)
    for s in range(n_slabs):
        cb_o[:, lanes[s]] = cbo_s[s]
        za_o[:, lanes[s]] = (ab[:, lanes[s]] * ca_s[s]).astype(BF16)


def _in_proj(x, g_mix, w_in, conv_a_w, conv_b_w, conv_b_bias, *, tm, tn):
    b, s, d = x.shape
    w = d // 2
    n_slabs = tn // LANES
    hb = tm // HALO
    n_hb = s // HALO
    grid = (b, s // tm, w // tn)

    def wspec(col0, width):
        blk0 = col0 // width
        return pl.BlockSpec((d, width), lambda bi, i, j: (0, blk0 + j))

    def cspec(rows):
        return pl.BlockSpec((rows, tn), lambda bi, i, j: (0, j))

    in_specs = [
        pl.BlockSpec((None, tm, d), lambda bi, i, j: (bi, i, 0)),
        pl.BlockSpec((None, HALO, d), lambda bi, i, j: (bi, jnp.maximum(i * hb - 1, 0), 0)),
        pl.BlockSpec((None, HALO, d), lambda bi, i, j: (bi, jnp.minimum((i + 1) * hb, n_hb - 1), 0)),
        _resident((1, d), lambda bi, i, j: (0, 0)),
        wspec(0, tn), wspec(w, tn), wspec(2 * w, tn), wspec(3 * w, tn), wspec(4 * w, tn),
        wspec(5 * w, 2 * tn), wspec(7 * w, 2 * tn),
        cspec(K_A), cspec(K_B), cspec(1),
    ]
    nspec = pl.BlockSpec((None, tm, tn), lambda bi, i, j: (bi, i, j))
    wide = pl.BlockSpec((None, tm, 2 * tn), lambda bi, i, j: (bi, i, j))
    out_shape = [jax.ShapeDtypeStruct((b, s, w), BF16), jax.ShapeDtypeStruct((b, s, w), F32),
                 jax.ShapeDtypeStruct((b, s, d), BF16), jax.ShapeDtypeStruct((b, s, d), BF16)]
    ext = pltpu.VMEM((n_slabs, tm + 2 * HALO, LANES), F32)
    out = pltpu.VMEM((n_slabs, tm, LANES), F32)
    return pl.pallas_call(
        _in_proj_kernel,
        grid=grid,
        in_specs=in_specs,
        out_specs=[nspec, nspec, wide, wide],
        out_shape=out_shape,
        scratch_shapes=[pltpu.VMEM((tm + 2 * HALO, d), BF16), ext, ext, out, out,
                        pltpu.VMEM((K_A, n_slabs, SUBLANES, LANES), F32),
                        pltpu.VMEM((K_B, n_slabs, SUBLANES, LANES), F32)],
        compiler_params=_params(3),
        name="in_proj",
    )(x, x, x, g_mix, *([w_in] * 7), conv_a_w, conv_b_w, conv_b_bias)


def _mixer_kernel(x_ref, za_ref, cb_ref, sga_ref, sgb_ref, lng_ref, lnb_ref,
                  woa_ref, wob_ref, wo_ref, o_ref):
    cb = cb_ref[...]
    cen = cb - jnp.mean(cb, axis=-1, keepdims=True)
    var = jnp.mean(cen * cen, axis=-1, keepdims=True)
    ln = cen * lax.rsqrt(var + EPS) * lng_ref[...] + lnb_ref[...]
    zb = (ln * jax.nn.sigmoid(ln)).astype(BF16)
    ya = _dot(za_ref[...], woa_ref[...])
    yb = _dot(zb, wob_ref[...])
    merged = (sga_ref[...].astype(F32) * ya + sgb_ref[...].astype(F32) * yb).astype(BF16)
    o_ref[...] = x_ref[...] + _dot(merged, wo_ref[...])


def _mixer(x, za, cb, sga, sgb, ln_g, ln_b, w_out_a, w_out_b, w_o, *, tm):
    b, s, d = x.shape
    w = d // 2

    def tile(c):
        return pl.BlockSpec((None, tm, c), lambda bi, i: (bi, i, 0))

    def const(shape):
        return _resident(shape, lambda bi, i: (0,) * len(shape))

    return pl.pallas_call(
        _mixer_kernel,
        grid=(b, s // tm),
        in_specs=[tile(d), tile(w), tile(w), tile(d), tile(d), const((1, w)), const((1, w)),
                  const((w, d)), const((w, d)), const((d, d))],
        out_specs=tile(d),
        out_shape=jax.ShapeDtypeStruct((b, s, d), F32),
        compiler_params=_params(2),
        name="mixer",
    )(x, za, cb, sga, sgb, ln_g, ln_b, w_out_a, w_out_b, w_o)


def _kv_kernel(mem_ref, g_ref, wk_ref, wv_ref, kt_o, v_o, mn_s):
    @pl.when(pl.program_id(0) == 0)
    def _():
        mn_s[...] = _rms(mem_ref[...], g_ref[...]).astype(BF16)

    n_mem = kt_o.shape[2]
    mn = mn_s[...]
    k = _dot(mn, wk_ref[...])
    for bi in range(kt_o.shape[0]):
        kt_o[bi] = k[bi * n_mem:(bi + 1) * n_mem, :].T.astype(BF16)
    v_o[...] = _dot(mn, wv_ref[...]).astype(BF16)


def _kv_proj(mem, g_mem, w_kv, *, tn):
    b, n_mem, d = mem.shape
    nj = d // tn
    rows = b * n_mem
    kt, v = pl.pallas_call(
        _kv_kernel,
        grid=(nj,),
        in_specs=[
            _resident((rows, d), lambda j: (0, 0)),
            _resident((1, d), lambda j: (0, 0)),
            pl.BlockSpec((d, tn), lambda j: (0, j)),
            pl.BlockSpec((d, tn), lambda j: (0, nj + j)),
        ],
        out_specs=[pl.BlockSpec((b, tn, n_mem), lambda j: (0, j, 0)),
                   pl.BlockSpec((rows, tn), lambda j: (0, j))],
        out_shape=[jax.ShapeDtypeStruct((b, d, n_mem), BF16), jax.ShapeDtypeStruct((rows, d), BF16)],
        scratch_shapes=[pltpu.VMEM((rows, d), BF16)],
        compiler_params=_params(1),
        name="kv_proj",
    )(mem.reshape(rows, d), g_mem, w_kv, w_kv)
    return kt, v.reshape(b, n_mem, d)


def _xattn_kernel(x_ref, g_ref, gn_ref, wq_ref, kt_ref, v_ref, wxo_ref, o_ref, hn_o, hn_s):
    d = x_ref.shape[1]
    hd = d // N_XHEADS
    scale = hd ** -0.5
    cols = [slice(h * hd, (h + 1) * hd) for h in range(N_XHEADS)]
    hn_s[...] = _rms(x_ref[...], g_ref[...]).astype(BF16)

    def query(h):
        return _dot(hn_s[...], wq_ref[:, cols[h]]).astype(BF16)

    def probs(q, h):
        s = _dot(q, kt_ref[cols[h], :]) * scale
        p = jnp.exp(s - jnp.max(s, axis=-1, keepdims=True))
        return (p / jnp.sum(p, axis=-1, keepdims=True)).astype(BF16)

    def attend(p, h):
        return _dot(p, v_ref[:, cols[h]]).astype(BF16)

    def project(o, h):
        return _dot(o, wxo_ref[cols[h], :])

    n = N_XHEADS
    q = [query(h) for h in range(min(2, n))]
    acc = x_ref[...]
    o_prev = None
    for h in range(n):
        p = probs(q[h], h)
        if h + 2 < n:
            q.append(query(h + 2))
        if o_prev is not None:
            acc = acc + project(o_prev, h - 1)
        o_prev = attend(p, h)
    acc = acc + project(o_prev, n - 1)
    o_ref[...] = acc
    hn_o[...] = _rms(acc, gn_ref[...]).astype(BF16)


def _xattn(x, g_xattn, g_next, w_q, kt, v, w_xo, *, tm):
    b, s, d = x.shape
    n_mem = v.shape[1]
    tile = pl.BlockSpec((None, tm, d), lambda bi, i: (bi, i, 0))
    return pl.pallas_call(
        _xattn_kernel,
        grid=(b, s // tm),
        in_specs=[
            tile,
            _resident((1, d), lambda bi, i: (0, 0)),
            _resident((1, d), lambda bi, i: (0, 0)),
            _resident((d, d), lambda bi, i: (0, 0)),
            pl.BlockSpec((None, d, n_mem), lambda bi, i: (bi, 0, 0)),
            pl.BlockSpec((None, n_mem, d), lambda bi, i: (bi, 0, 0)),
            _resident((d, d), lambda bi, i: (0, 0)),
        ],
        out_specs=[tile, tile],
        out_shape=[jax.ShapeDtypeStruct((b, s, d), F32), jax.ShapeDtypeStruct((b, s, d), BF16)],
        scratch_shapes=[pltpu.VMEM((tm, d), BF16)],
        compiler_params=_params(2),
        name="xattn",
    )(x, g_xattn, g_next, w_q, kt, v, w_xo)


def _ffn_kernel(hn_ref, x_hbm, wg_ref, wu_ref, wd_ref, gf_ref, o_ref, x_s, sem):
    tm = hn_ref.shape[0]
    i = pl.program_id(0)
    j = pl.program_id(1)
    fetch_x = pltpu.make_async_copy(x_hbm.at[pl.ds(i * tm, tm), :], x_s, sem)

    last = pl.num_programs(1) - 1

    def partial_sum():
        hn = hn_ref[...]
        gt = _dot(hn, wg_ref[...])
        up = _dot(hn, wu_ref[...])
        return _dot((gt * jax.nn.sigmoid(gt) * up).astype(BF16), wd_ref[...])

    @pl.when(j == 0)
    def _():
        fetch_x.start()
        o_ref[...] = partial_sum()

    @pl.when(jnp.logical_and(j > 0, j < last))
    def _():
        o_ref[...] += partial_sum()

    @pl.when(j == last)
    def _():
        fetch_x.wait()
        y = x_s[...] + o_ref[...] + partial_sum()
        o_ref[...] = y
        ms = jnp.mean(y * y, axis=-1, keepdims=True)
        o_ref[...] = o_ref[...] * lax.rsqrt(ms + EPS) * gf_ref[...]


def _ffn(hn2d, x2d, w_gate_up, w_down, g_final, *, tm, tf):
    t, d = x2d.shape
    d_ff = w_down.shape[0]
    nj = d_ff // tf
    assert nj >= 2, "first and last d_ff steps must be distinct grid steps"
    tile = pl.BlockSpec((tm, d), lambda i, j: (i, 0))
    return pl.pallas_call(
        _ffn_kernel,
        grid=(t // tm, nj),
        in_specs=[
            tile,
            pl.BlockSpec(memory_space=pl.ANY),
            pl.BlockSpec((d, tf), lambda i, j: (0, j)),
            pl.BlockSpec((d, tf), lambda i, j: (0, nj + j)),
            pl.BlockSpec((tf, d), lambda i, j: (j, 0)),
            _resident((1, d), lambda i, j: (0, 0)),
        ],
        out_specs=tile,
        out_shape=jax.ShapeDtypeStruct((t, d), F32),
        scratch_shapes=[pltpu.VMEM((tm, d), F32), pltpu.SemaphoreType.DMA(())],
        compiler_params=_params(2),
        name="ffn",
    )(hn2d, x2d, w_gate_up, w_gate_up, w_down, g_final)


def _tile(n, want):
    for t in range(min(want, n), 15, -1):
        if n % t == 0 and t % 16 == 0:
            return t
    return n


def _trunk(x, mem, p):
    b, s, d = x.shape
    w = d // 2
    t = b * s
    row = lambda a: a.reshape(1, -1)

    za, cb, sga, sgb = _in_proj(x, row(p["g_mix"]), p["w_in"], p["conv_a_w"], p["conv_b_w"],
                                row(p["conv_b_bias"]), tm=_tile(s, 1024), tn=_tile(w, 256))
    x1 = _mixer(x, za, cb, sga, sgb, row(p["ln_b_g"]), row(p["ln_b_b"]),
                p["w_out_a"], p["w_out_b"], p["w_o"], tm=_tile(s, 512))
    kt, v = _kv_proj(mem, row(p["g_mem"]), p["w_kv"], tn=_tile(d, 512))
    x2, hn = _xattn(x1, row(p["g_xattn"]), row(p["g_ffn"]), p["w_q"], kt, v, p["w_xo"], tm=_tile(s, 512))
    d_ff = p["w_down"].shape[0]
    y = _ffn(hn.reshape(t, d), x2.reshape(t, d), p["w_gate_up"], p["w_down"], row(p["g_final"]),
             tm=_tile(t, 1024), tf=_tile(d_ff, 512))
    return y.reshape(b, s, d)


def kernel(x_prompt, x_sample, mem_prompt, mem_sample, g_mix, w_in, conv_a_w, w_out_a, conv_b_w,
           conv_b_bias, ln_b_g, ln_b_b, w_out_b, w_o, g_xattn, g_mem, w_q, w_kv, w_xo, g_ffn,
           w_gate_up, w_down, g_final):
    assert g_mix.shape[0] == 1, "single-layer trunk"
    p = dict(
        g_mix=g_mix[0], conv_a_w=conv_a_w[0], conv_b_w=conv_b_w[0], conv_b_bias=conv_b_bias[0],
        ln_b_g=ln_b_g[0], ln_b_b=ln_b_b[0], g_xattn=g_xattn[0], g_mem=g_mem[0], g_ffn=g_ffn[0],
        g_final=g_final,
        w_in=w_in[0].astype(BF16), w_out_a=w_out_a[0].astype(BF16), w_out_b=w_out_b[0].astype(BF16),
        w_o=w_o[0].astype(BF16), w_q=w_q[0].astype(BF16), w_kv=w_kv[0].astype(BF16),
        w_xo=w_xo[0].astype(BF16), w_gate_up=w_gate_up[0].astype(BF16), w_down=w_down[0].astype(BF16),
    )
    return (_trunk(x_prompt, mem_prompt, p), _trunk(x_sample, mem_sample, p))
```

```python
import math

import jax
import jax.numpy as jnp
from jax import lax
from jax.experimental import pallas as pl
from jax.experimental.pallas import tpu as pltpu

EPS = 1e-6
N_XHEADS = 4
K_A = 3
K_B = 31
HALO = 16
LANES = 128
SUBLANES = 8
V7X_VMEM_LIMIT_BYTES = 60 * 1024 * 1024

F32 = jnp.float32
BF16 = jnp.bfloat16


def _params(n_grid):
    return pltpu.CompilerParams(
        dimension_semantics=("arbitrary",) * n_grid,
        vmem_limit_bytes=V7X_VMEM_LIMIT_BYTES)


def _resident(shape, index_map):
    return pl.BlockSpec(shape, index_map, pipeline_mode=pl.Buffered(1))


def _rms(x, g):
    return x * lax.rsqrt(jnp.mean(x * x, axis=-1, keepdims=True) + EPS) * g


def _dot(a, b):
    return jnp.dot(a, b, preferred_element_type=F32)


def _row_walk_specs(weights, grid):
    n_steps = math.prod(grid)
    strides = [math.prod(grid[k + 1:]) for k in range(len(grid))]
    specs = []
    for wgt in weights:
        r, c = wgt.shape
        n_blocks = max(n for n in range(1, n_steps + 1) if r % n == 0 and (r // n) % 16 == 0)

        def index_map(*g, n_blocks=n_blocks):
            step = sum(gi * st for gi, st in zip(g, strides))
            return (jnp.minimum(step, n_blocks - 1), 0)

        specs.append(pl.BlockSpec((r // n_blocks, c), index_map))
    return specs


def _pallas_call(body, *, grid, in_specs, out_specs, out_shape, scratch_shapes=(), name, casts=()):
    n_in, n_out, n_cast = len(in_specs), len(out_specs), len(casts)
    cast_specs = _row_walk_specs(casts, grid)

    def kern(*refs):
        ins, rest = refs[:n_in], refs[n_in:]
        cast_in, rest = rest[:n_cast], rest[n_cast:]
        outs, rest = rest[:n_out], rest[n_out:]
        cast_out, scratch = rest[:n_cast], rest[n_cast:]
        body(*ins, *outs, *scratch)
        for src, dst in zip(cast_in, cast_out):
            dst[...] = src[...].astype(BF16)

    call = pl.pallas_call(
        kern,
        grid=grid,
        in_specs=[*in_specs, *cast_specs],
        out_specs=[*out_specs, *cast_specs],
        out_shape=[*out_shape, *(jax.ShapeDtypeStruct(c.shape, BF16) for c in casts)],
        scratch_shapes=list(scratch_shapes),
        compiler_params=_params(len(grid)),
        name=name,
    )

    def run(*operands):
        res = call(*operands, *casts)
        return res[:n_out], res[n_out:]

    return run


def _depthwise_conv(ext_s, w_s, out_s, init, n_taps, tm):
    n_slabs = ext_s.shape[0]
    span = 2 * SUBLANES
    off = HALO - n_taps // 2
    for s in range(n_slabs):
        for r0 in range(0, tm, span):
            acc = [init[s], init[s]]
            for k in range(n_taps):
                w = w_s[k, s]
                for ph in range(2):
                    acc[ph] = acc[ph] + w * ext_s[s, pl.ds(r0 + ph + off + k, SUBLANES, stride=2), :]
            for ph in range(2):
                out_s[s, pl.ds(r0 + ph, SUBLANES, stride=2), :] = acc[ph]


def _in_proj_kernel(x_ref, xp_ref, xn_ref, g_ref, wab, wac, wav, wbv, wbg, wga, wgb,
                    caw_ref, cbw_ref, cbb_ref,
                    za_o, cb_o, sga_o, sgb_o,
                    xe_s, exta_s, extb_s, ca_s, cbo_s, wa_s, wb_s):
    tm = x_ref.shape[0]
    n_slabs = exta_s.shape[0]
    i = pl.program_id(1)

    @pl.when(pl.program_id(2) == 0)
    def _():
        g = g_ref[...]
        inside_prev = i > 0
        inside_next = i < pl.num_programs(1) - 1
        xe_s[0:HALO, :] = jnp.where(inside_prev, _rms(xp_ref[...], g), 0.0).astype(BF16)
        xe_s[HALO:HALO + tm, :] = _rms(x_ref[...], g).astype(BF16)
        xe_s[HALO + tm:HALO + tm + HALO, :] = jnp.where(inside_next, _rms(xn_ref[...], g), 0.0).astype(BF16)

    lanes = [slice(s * LANES, (s + 1) * LANES) for s in range(n_slabs)]
    bias = []
    for s in range(n_slabs):
        for k in range(K_A):
            wa_s[k, s] = jnp.broadcast_to(caw_ref[k:k + 1, lanes[s]], (SUBLANES, LANES))
        for k in range(K_B):
            wb_s[k, s] = jnp.broadcast_to(cbw_ref[k:k + 1, lanes[s]], (SUBLANES, LANES))
        bias.append(jnp.broadcast_to(cbb_ref[0:1, lanes[s]], (SUBLANES, LANES)))
    zero = [jnp.zeros((SUBLANES, LANES), F32)] * n_slabs

    xe = xe_s[...]
    xc = xe_s[HALO:HALO + tm, :]
    u = _dot(xe, wbv[...]) * jax.nn.sigmoid(_dot(xe, wbg[...]))
    cv = _dot(xe, wac[...]) * _dot(xe, wav[...])
    for s in range(n_slabs):
        extb_s[s] = u[:, lanes[s]]
        exta_s[s] = cv[:, lanes[s]]
    sga_o[...] = jax.nn.sigmoid(_dot(xc, wga[...])).astype(BF16)
    sgb_o[...] = jax.nn.sigmoid(_dot(xc, wgb[...])).astype(BF16)
    ab = _dot(xc, wab[...])

    _depthwise_conv(extb_s, wb_s, cbo_s, bias, K_B, tm)
    _depthwise_conv(exta_s, wa_s, ca_s, zero, K_A, tm)
    for s in range(n_slabs):
        cb_o[:, lanes[s]] = cbo_s[s]
        za_o[:, lanes[s]] = (ab[:, lanes[s]] * ca_s[s]).astype(BF16)


def _in_proj(x, g_mix, w_in, conv_a_w, conv_b_w, conv_b_bias, *, tm, tn, casts=()):
    b, s, d = x.shape
    w = d // 2
    n_slabs = tn // LANES
    hb = tm // HALO
    n_hb = s // HALO
    grid = (b, s // tm, w // tn)

    def wspec(col0, width):
        blk0 = col0 // width
        return pl.BlockSpec((d, width), lambda bi, i, j: (0, blk0 + j))

    def cspec(rows):
        return pl.BlockSpec((rows, tn), lambda bi, i, j: (0, j))

    in_specs = [
        pl.BlockSpec((None, tm, d), lambda bi, i, j: (bi, i, 0)),
        pl.BlockSpec((None, HALO, d), lambda bi, i, j: (bi, jnp.maximum(i * hb - 1, 0), 0)),
        pl.BlockSpec((None, HALO, d), lambda bi, i, j: (bi, jnp.minimum((i + 1) * hb, n_hb - 1), 0)),
        _resident((1, d), lambda bi, i, j: (0, 0)),
        wspec(0, tn), wspec(w, tn), wspec(2 * w, tn), wspec(3 * w, tn), wspec(4 * w, tn),
        wspec(5 * w, 2 * tn), wspec(7 * w, 2 * tn),
        cspec(K_A), cspec(K_B), cspec(1),
    ]
    nspec = pl.BlockSpec((None, tm, tn), lambda bi, i, j: (bi, i, j))
    wide = pl.BlockSpec((None, tm, 2 * tn), lambda bi, i, j: (bi, i, j))
    out_shape = [jax.ShapeDtypeStruct((b, s, w), BF16), jax.ShapeDtypeStruct((b, s, w), F32),
                 jax.ShapeDtypeStruct((b, s, d), BF16), jax.ShapeDtypeStruct((b, s, d), BF16)]
    ext = pltpu.VMEM((n_slabs, tm + 2 * HALO, LANES), F32)
    out = pltpu.VMEM((n_slabs, tm, LANES), F32)
    return _pallas_call(
        _in_proj_kernel,
        grid=grid,
        in_specs=in_specs,
        out_specs=[nspec, nspec, wide, wide],
        out_shape=out_shape,
        scratch_shapes=[pltpu.VMEM((tm + 2 * HALO, d), BF16), ext, ext, out, out,
                        pltpu.VMEM((K_A, n_slabs, SUBLANES, LANES), F32),
                        pltpu.VMEM((K_B, n_slabs, SUBLANES, LANES), F32)],
        name="in_proj",
        casts=casts,
    )(x, x, x, g_mix, *([w_in] * 7), conv_a_w, conv_b_w, conv_b_bias)


def _mixer_kernel(x_ref, za_ref, cb_ref, sga_ref, sgb_ref, lng_ref, lnb_ref,
                  woa_ref, wob_ref, wo_ref, o_ref):
    cb = cb_ref[...]
    cen = cb - jnp.mean(cb, axis=-1, keepdims=True)
    var = jnp.mean(cen * cen, axis=-1, keepdims=True)
    ln = cen * lax.rsqrt(var + EPS) * lng_ref[...] + lnb_ref[...]
    zb = (ln * jax.nn.sigmoid(ln)).astype(BF16)
    ya = _dot(za_ref[...], woa_ref[...])
    yb = _dot(zb, wob_ref[...])
    merged = (sga_ref[...].astype(F32) * ya + sgb_ref[...].astype(F32) * yb).astype(BF16)
    o_ref[...] = x_ref[...] + _dot(merged, wo_ref[...])


def _mixer(x, za, cb, sga, sgb, ln_g, ln_b, w_out_a, w_out_b, w_o, *, tm, casts=()):
    b, s, d = x.shape
    w = d // 2

    def tile(c):
        return pl.BlockSpec((None, tm, c), lambda bi, i: (bi, i, 0))

    def const(shape):
        return _resident(shape, lambda bi, i: (0,) * len(shape))

    return _pallas_call(
        _mixer_kernel,
        grid=(b, s // tm),
        in_specs=[tile(d), tile(w), tile(w), tile(d), tile(d), const((1, w)), const((1, w)),
                  const((w, d)), const((w, d)), const((d, d))],
        out_specs=[tile(d)],
        out_shape=[jax.ShapeDtypeStruct((b, s, d), F32)],
        name="mixer",
        casts=casts,
    )(x, za, cb, sga, sgb, ln_g, ln_b, w_out_a, w_out_b, w_o)


def _kv_kernel(mem_ref, g_ref, wk_ref, wv_ref, kt_o, v_o, mn_s):
    @pl.when(pl.program_id(0) == 0)
    def _():
        mn_s[...] = _rms(mem_ref[...], g_ref[...]).astype(BF16)

    n_mem = kt_o.shape[2]
    mn = mn_s[...]
    k = _dot(mn, wk_ref[...])
    for bi in range(kt_o.shape[0]):
        kt_o[bi] = k[bi * n_mem:(bi + 1) * n_mem, :].T.astype(BF16)
    v_o[...] = _dot(mn, wv_ref[...]).astype(BF16)


def _kv_proj(mem, g_mem, w_kv, *, tn):
    b, n_mem, d = mem.shape
    nj = d // tn
    rows = b * n_mem
    kt, v = pl.pallas_call(
        _kv_kernel,
        grid=(nj,),
        in_specs=[
            _resident((rows, d), lambda j: (0, 0)),
            _resident((1, d), lambda j: (0, 0)),
            pl.BlockSpec((d, tn), lambda j: (0, j)),
            pl.BlockSpec((d, tn), lambda j: (0, nj + j)),
        ],
        out_specs=[pl.BlockSpec((b, tn, n_mem), lambda j: (0, j, 0)),
                   pl.BlockSpec((rows, tn), lambda j: (0, j))],
        out_shape=[jax.ShapeDtypeStruct((b, d, n_mem), BF16), jax.ShapeDtypeStruct((rows, d), BF16)],
        scratch_shapes=[pltpu.VMEM((rows, d), BF16)],
        compiler_params=_params(1),
        name="kv_proj",
    )(mem.reshape(rows, d), g_mem, w_kv, w_kv)
    return kt, v.reshape(b, n_mem, d)


def _xattn_kernel(x_ref, g_ref, gn_ref, wq_ref, kt_ref, v_ref, wxo_ref, o_ref, hn_o, hn_s):
    d = x_ref.shape[1]
    hd = d // N_XHEADS
    scale = hd ** -0.5
    cols = [slice(h * hd, (h + 1) * hd) for h in range(N_XHEADS)]
    hn_s[...] = _rms(x_ref[...], g_ref[...]).astype(BF16)

    def query(h):
        return _dot(hn_s[...], wq_ref[:, cols[h]]).astype(BF16)

    def probs(q, h):
        s = _dot(q, kt_ref[cols[h], :]) * scale
        p = jnp.exp(s - jnp.max(s, axis=-1, keepdims=True))
        return (p / jnp.sum(p, axis=-1, keepdims=True)).astype(BF16)

    def attend(p, h):
        return _dot(p, v_ref[:, cols[h]]).astype(BF16)

    def project(o, h):
        return _dot(o, wxo_ref[cols[h], :])

    n = N_XHEADS
    q = [query(h) for h in range(min(2, n))]
    acc = x_ref[...]
    o_prev = None
    for h in range(n):
        p = probs(q[h], h)
        if h + 2 < n:
            q.append(query(h + 2))
        if o_prev is not None:
            acc = acc + project(o_prev, h - 1)
        o_prev = attend(p, h)
    acc = acc + project(o_prev, n - 1)
    o_ref[...] = acc
    hn_o[...] = _rms(acc, gn_ref[...]).astype(BF16)


def _xattn(x, g_xattn, g_next, w_q, kt, v, w_xo, *, tm, casts=()):
    b, s, d = x.shape
    n_mem = v.shape[1]
    tile = pl.BlockSpec((None, tm, d), lambda bi, i: (bi, i, 0))
    return _pallas_call(
        _xattn_kernel,
        grid=(b, s // tm),
        in_specs=[
            tile,
            _resident((1, d), lambda bi, i: (0, 0)),
            _resident((1, d), lambda bi, i: (0, 0)),
            _resident((d, d), lambda bi, i: (0, 0)),
            pl.BlockSpec((None, d, n_mem), lambda bi, i: (bi, 0, 0)),
            pl.BlockSpec((None, n_mem, d), lambda bi, i: (bi, 0, 0)),
            _resident((d, d), lambda bi, i: (0, 0)),
        ],
        out_specs=[tile, tile],
        out_shape=[jax.ShapeDtypeStruct((b, s, d), F32), jax.ShapeDtypeStruct((b, s, d), BF16)],
        scratch_shapes=[pltpu.VMEM((tm, d), BF16)],
        name="xattn",
        casts=casts,
    )(x, g_xattn, g_next, w_q, kt, v, w_xo)


def _ffn_kernel(hn_ref, x_hbm, wg_ref, wu_ref, wd_ref, gf_ref, o_ref, x_s, sem):
    tm = hn_ref.shape[0]
    i = pl.program_id(0)
    j = pl.program_id(1)
    fetch_x = pltpu.make_async_copy(x_hbm.at[pl.ds(i * tm, tm), :], x_s, sem)

    last = pl.num_programs(1) - 1

    def partial_sum():
        hn = hn_ref[...]
        gt = _dot(hn, wg_ref[...])
        up = _dot(hn, wu_ref[...])
        return _dot((gt * jax.nn.sigmoid(gt) * up).astype(BF16), wd_ref[...])

    @pl.when(j == 0)
    def _():
        fetch_x.start()
        o_ref[...] = partial_sum()

    @pl.when(jnp.logical_and(j > 0, j < last))
    def _():
        o_ref[...] += partial_sum()

    @pl.when(j == last)
    def _():
        fetch_x.wait()
        y = x_s[...] + o_ref[...] + partial_sum()
        o_ref[...] = y
        ms = jnp.mean(y * y, axis=-1, keepdims=True)
        o_ref[...] = o_ref[...] * lax.rsqrt(ms + EPS) * gf_ref[...]


def _ffn(hn2d, x2d, w_gate_up, w_down, g_final, *, tm, tf):
    t, d = x2d.shape
    d_ff = w_down.shape[0]
    nj = d_ff // tf
    assert nj >= 2, "first and last d_ff steps must be distinct grid steps"
    tile = pl.BlockSpec((tm, d), lambda i, j: (i, 0))
    return pl.pallas_call(
        _ffn_kernel,
        grid=(t // tm, nj),
        in_specs=[
            tile,
            pl.BlockSpec(memory_space=pl.ANY),
            pl.BlockSpec((d, tf), lambda i, j: (0, j)),
            pl.BlockSpec((d, tf), lambda i, j: (0, nj + j)),
            pl.BlockSpec((tf, d), lambda i, j: (j, 0)),
            _resident((1, d), lambda i, j: (0, 0)),
        ],
        out_specs=tile,
        out_shape=jax.ShapeDtypeStruct((t, d), F32),
        scratch_shapes=[pltpu.VMEM((tm, d), F32), pltpu.SemaphoreType.DMA(())],
        compiler_params=_params(2),
        name="ffn",
    )(hn2d, x2d, w_gate_up, w_gate_up, w_down, g_final)


def _tile(n, want):
    for t in range(min(want, n), 15, -1):
        if n % t == 0 and t % 16 == 0:
            return t
    return n


_CAST_IN_PROJ = ("w_out_a", "w_out_b", "w_o")
_CAST_MIXER = ("w_q", "w_xo", "w_kv")
_CAST_XATTN = ("w_gate_up", "w_down")


def _trunk(x, mem, p, f32_weights=None):
    b, s, d = x.shape
    w = d // 2
    t = b * s
    row = lambda a: a.reshape(1, -1)
    p = dict(p)

    def casts(names):
        return [f32_weights[k] for k in names] if f32_weights else []

    (za, cb, sga, sgb), done = _in_proj(x, row(p["g_mix"]), p["w_in"], p["conv_a_w"], p["conv_b_w"],
                                        row(p["conv_b_bias"]), tm=_tile(s, 1024), tn=_tile(w, 256),
                                        casts=casts(_CAST_IN_PROJ))
    p.update(zip(_CAST_IN_PROJ, done))
    (x1,), done = _mixer(x, za, cb, sga, sgb, row(p["ln_b_g"]), row(p["ln_b_b"]),
                         p["w_out_a"], p["w_out_b"], p["w_o"], tm=_tile(s, 512), casts=casts(_CAST_MIXER))
    p.update(zip(_CAST_MIXER, done))
    kt, v = _kv_proj(mem, row(p["g_mem"]), p["w_kv"], tn=_tile(d, 512))
    (x2, hn), done = _xattn(x1, row(p["g_xattn"]), row(p["g_ffn"]), p["w_q"], kt, v, p["w_xo"],
                            tm=_tile(s, 512), casts=casts(_CAST_XATTN))
    p.update(zip(_CAST_XATTN, done))
    d_ff = p["w_down"].shape[0]
    y = _ffn(hn.reshape(t, d), x2.reshape(t, d), p["w_gate_up"], p["w_down"], row(p["g_final"]),
             tm=_tile(t, 1024), tf=_tile(d_ff, 512))
    return y.reshape(b, s, d), p


def kernel(x_prompt, x_sample, mem_prompt, mem_sample, g_mix, w_in, conv_a_w, w_out_a, conv_b_w,
           conv_b_bias, ln_b_g, ln_b_b, w_out_b, w_o, g_xattn, g_mem, w_q, w_kv, w_xo, g_ffn,
           w_gate_up, w_down, g_final):
    assert g_mix.shape[0] == 1, "single-layer trunk"
    p = dict(
        g_mix=g_mix[0], conv_a_w=conv_a_w[0], conv_b_w=conv_b_w[0], conv_b_bias=conv_b_bias[0],
        ln_b_g=ln_b_g[0], ln_b_b=ln_b_b[0], g_xattn=g_xattn[0], g_mem=g_mem[0], g_ffn=g_ffn[0],
        g_final=g_final,
        w_in=w_in[0].astype(BF16),
    )
    f32_weights = dict(w_out_a=w_out_a[0], w_out_b=w_out_b[0], w_o=w_o[0], w_q=w_q[0], w_kv=w_kv[0],
                       w_xo=w_xo[0], w_gate_up=w_gate_up[0], w_down=w_down[0])
    y_prompt, p = _trunk(x_prompt, mem_prompt, p, f32_weights)
    y_sample, _ = _trunk(x_sample, mem_sample, p)
    return (y_prompt, y_sample)
```

```python
import math

import jax
import jax.numpy as jnp
from jax import lax
from jax.experimental import pallas as pl
from jax.experimental.pallas import tpu as pltpu

EPS = 1e-6
N_XHEADS = 4
K_A = 3
K_B = 31
HALO = 16
LANES = 128
SUBLANES = 8
FFN_LAST_STEP_CHUNKS = 4
V7X_VMEM_LIMIT_BYTES = 60 * 1024 * 1024

F32 = jnp.float32
BF16 = jnp.bfloat16


def _params(n_grid):
    return pltpu.CompilerParams(
        dimension_semantics=("arbitrary",) * n_grid,
        vmem_limit_bytes=V7X_VMEM_LIMIT_BYTES)


def _resident(shape, index_map):
    return pl.BlockSpec(shape, index_map, pipeline_mode=pl.Buffered(1))


def _rms(x, g):
    return x * lax.rsqrt(jnp.mean(x * x, axis=-1, keepdims=True) + EPS) * g


def _dot(a, b):
    return jnp.dot(a, b, preferred_element_type=F32)


def _split_rms(x, g):
    return (x * g).astype(BF16), lax.rsqrt(jnp.mean(x * x, axis=-1, keepdims=True) + EPS)


def _row_walk_specs(weights, grid):
    n_steps = math.prod(grid)
    strides = [math.prod(grid[k + 1:]) for k in range(len(grid))]
    specs = []
    for wgt in weights:
        r, c = wgt.shape
        n_blocks = max(n for n in range(1, n_steps + 1) if r % n == 0 and (r // n) % 16 == 0)

        def index_map(*g, n_blocks=n_blocks):
            step = sum(gi * st for gi, st in zip(g, strides))
            return (jnp.minimum(step, n_blocks - 1), 0)

        specs.append(pl.BlockSpec((r // n_blocks, c), index_map))
    return specs


def _pallas_call(body, *, grid, in_specs, out_specs, out_shape, scratch_shapes=(), name, casts=()):
    n_in, n_out, n_cast = len(in_specs), len(out_specs), len(casts)
    cast_specs = _row_walk_specs(casts, grid)

    def kern(*refs):
        ins, rest = refs[:n_in], refs[n_in:]
        cast_in, rest = rest[:n_cast], rest[n_cast:]
        outs, rest = rest[:n_out], rest[n_out:]
        cast_out, scratch = rest[:n_cast], rest[n_cast:]
        body(*ins, *outs, *scratch)
        for src, dst in zip(cast_in, cast_out):
            dst[...] = src[...].astype(BF16)

    call = pl.pallas_call(
        kern,
        grid=grid,
        in_specs=[*in_specs, *cast_specs],
        out_specs=[*out_specs, *cast_specs],
        out_shape=[*out_shape, *(jax.ShapeDtypeStruct(c.shape, BF16) for c in casts)],
        scratch_shapes=list(scratch_shapes),
        compiler_params=_params(len(grid)),
        name=name,
    )

    def run(*operands):
        res = call(*operands, *casts)
        return res[:n_out], res[n_out:]

    return run


def _depthwise_conv(ext_s, w_s, out_s, init, n_taps, tm):
    n_slabs = ext_s.shape[0]
    span = 2 * SUBLANES
    off = HALO - n_taps // 2
    for s in range(n_slabs):
        for r0 in range(0, tm, span):
            acc = [init[s], init[s]]
            for k in range(n_taps):
                w = w_s[k, s]
                for ph in range(2):
                    acc[ph] = acc[ph] + w * ext_s[s, pl.ds(r0 + ph + off + k, SUBLANES, stride=2), :]
            for ph in range(2):
                out_s[s, pl.ds(r0 + ph, SUBLANES, stride=2), :] = acc[ph]


def _in_proj_kernel(x_ref, xp_ref, xn_ref, g_ref, wab, wac, wav, wbv, wbg, wga, wgb,
                    caw_ref, cbw_ref, cbb_ref,
                    za_o, cb_o, sga_o, sgb_o,
                    xe_s, exta_s, extb_s, ca_s, cbo_s, wa_s, wb_s):
    tm = x_ref.shape[0]
    n_slabs = exta_s.shape[0]
    i = pl.program_id(1)

    @pl.when(pl.program_id(2) == 0)
    def _():
        g = g_ref[...]
        inside_prev = i > 0
        inside_next = i < pl.num_programs(1) - 1
        xe_s[0:HALO, :] = jnp.where(inside_prev, _rms(xp_ref[...], g), 0.0).astype(BF16)
        xe_s[HALO:HALO + tm, :] = _rms(x_ref[...], g).astype(BF16)
        xe_s[HALO + tm:HALO + tm + HALO, :] = jnp.where(inside_next, _rms(xn_ref[...], g), 0.0).astype(BF16)

    lanes = [slice(s * LANES, (s + 1) * LANES) for s in range(n_slabs)]
    bias = []
    for s in range(n_slabs):
        for k in range(K_A):
            wa_s[k, s] = jnp.broadcast_to(caw_ref[k:k + 1, lanes[s]], (SUBLANES, LANES))
        for k in range(K_B):
            wb_s[k, s] = jnp.broadcast_to(cbw_ref[k:k + 1, lanes[s]], (SUBLANES, LANES))
        bias.append(jnp.broadcast_to(cbb_ref[0:1, lanes[s]], (SUBLANES, LANES)))
    zero = [jnp.zeros((SUBLANES, LANES), F32)] * n_slabs

    xe = xe_s[...]
    xc = xe_s[HALO:HALO + tm, :]
    u = _dot(xe, wbv[...]) * jax.nn.sigmoid(_dot(xe, wbg[...]))
    cv = _dot(xe, wac[...]) * _dot(xe, wav[...])
    for s in range(n_slabs):
        extb_s[s] = u[:, lanes[s]]
        exta_s[s] = cv[:, lanes[s]]
    sga_o[...] = jax.nn.sigmoid(_dot(xc, wga[...])).astype(BF16)
    sgb_o[...] = jax.nn.sigmoid(_dot(xc, wgb[...])).astype(BF16)
    ab = _dot(xc, wab[...])

    _depthwise_conv(extb_s, wb_s, cbo_s, bias, K_B, tm)
    _depthwise_conv(exta_s, wa_s, ca_s, zero, K_A, tm)
    for s in range(n_slabs):
        cb_o[:, lanes[s]] = cbo_s[s]
        za_o[:, lanes[s]] = (ab[:, lanes[s]] * ca_s[s]).astype(BF16)


def _in_proj(x, g_mix, w_in, conv_a_w, conv_b_w, conv_b_bias, *, tm, tn, casts=()):
    b, s, d = x.shape
    w = d // 2
    n_slabs = tn // LANES
    hb = tm // HALO
    n_hb = s // HALO
    grid = (b, s // tm, w // tn)

    def wspec(col0, width):
        blk0 = col0 // width
        return pl.BlockSpec((d, width), lambda bi, i, j: (0, blk0 + j))

    def cspec(rows):
        return pl.BlockSpec((rows, tn), lambda bi, i, j: (0, j))

    in_specs = [
        pl.BlockSpec((None, tm, d), lambda bi, i, j: (bi, i, 0)),
        pl.BlockSpec((None, HALO, d), lambda bi, i, j: (bi, jnp.maximum(i * hb - 1, 0), 0)),
        pl.BlockSpec((None, HALO, d), lambda bi, i, j: (bi, jnp.minimum((i + 1) * hb, n_hb - 1), 0)),
        _resident((1, d), lambda bi, i, j: (0, 0)),
        wspec(0, tn), wspec(w, tn), wspec(2 * w, tn), wspec(3 * w, tn), wspec(4 * w, tn),
        wspec(5 * w, 2 * tn), wspec(7 * w, 2 * tn),
        cspec(K_A), cspec(K_B), cspec(1),
    ]
    nspec = pl.BlockSpec((None, tm, tn), lambda bi, i, j: (bi, i, j))
    wide = pl.BlockSpec((None, tm, 2 * tn), lambda bi, i, j: (bi, i, j))
    out_shape = [jax.ShapeDtypeStruct((b, s, w), BF16), jax.ShapeDtypeStruct((b, s, w), F32),
                 jax.ShapeDtypeStruct((b, s, d), BF16), jax.ShapeDtypeStruct((b, s, d), BF16)]
    ext = pltpu.VMEM((n_slabs, tm + 2 * HALO, LANES), F32)
    out = pltpu.VMEM((n_slabs, tm, LANES), F32)
    return _pallas_call(
        _in_proj_kernel,
        grid=grid,
        in_specs=in_specs,
        out_specs=[nspec, nspec, wide, wide],
        out_shape=out_shape,
        scratch_shapes=[pltpu.VMEM((tm + 2 * HALO, d), BF16), ext, ext, out, out,
                        pltpu.VMEM((K_A, n_slabs, SUBLANES, LANES), F32),
                        pltpu.VMEM((K_B, n_slabs, SUBLANES, LANES), F32)],
        name="in_proj",
        casts=casts,
    )(x, x, x, g_mix, *([w_in] * 7), conv_a_w, conv_b_w, conv_b_bias)


def _mixer_kernel(x_ref, za_ref, cb_ref, sga_ref, sgb_ref, lng_ref, lnb_ref,
                  woa_ref, wob_ref, wo_ref, gn_ref, o_ref, xs_o, r_o):
    cb = cb_ref[...]
    cen = cb - jnp.mean(cb, axis=-1, keepdims=True)
    var = jnp.mean(cen * cen, axis=-1, keepdims=True)
    ln = cen * lax.rsqrt(var + EPS) * lng_ref[...] + lnb_ref[...]
    zb = (ln * jax.nn.sigmoid(ln)).astype(BF16)
    ya = _dot(za_ref[...], woa_ref[...])
    yb = _dot(zb, wob_ref[...])
    merged = (sga_ref[...].astype(F32) * ya + sgb_ref[...].astype(F32) * yb).astype(BF16)
    x1 = x_ref[...] + _dot(merged, wo_ref[...])
    o_ref[...] = x1
    xs_o[...], r_o[...] = _split_rms(x1, gn_ref[...])


def _mixer(x, za, cb, sga, sgb, ln_g, ln_b, w_out_a, w_out_b, w_o, g_next, *, tm, casts=()):
    b, s, d = x.shape
    w = d // 2

    def tile(c):
        return pl.BlockSpec((None, tm, c), lambda bi, i: (bi, i, 0))

    def const(shape):
        return _resident(shape, lambda bi, i: (0,) * len(shape))

    return _pallas_call(
        _mixer_kernel,
        grid=(b, s // tm),
        in_specs=[tile(d), tile(w), tile(w), tile(d), tile(d), const((1, w)), const((1, w)),
                  const((w, d)), const((w, d)), const((d, d)), const((1, d))],
        out_specs=[tile(d), tile(d), tile(1)],
        out_shape=[jax.ShapeDtypeStruct((b, s, d), F32), jax.ShapeDtypeStruct((b, s, d), BF16),
                   jax.ShapeDtypeStruct((b, s, 1), F32)],
        name="mixer",
        casts=casts,
    )(x, za, cb, sga, sgb, ln_g, ln_b, w_out_a, w_out_b, w_o, g_next)


def _kv_kernel(mem_ref, g_ref, wk_ref, wv_ref, kt_o, v_o, mn_s):
    @pl.when(pl.program_id(0) == 0)
    def _():
        mn_s[...] = _rms(mem_ref[...], g_ref[...]).astype(BF16)

    n_mem = kt_o.shape[2]
    mn = mn_s[...]
    k = _dot(mn, wk_ref[...])
    for bi in range(kt_o.shape[0]):
        kt_o[bi] = k[bi * n_mem:(bi + 1) * n_mem, :].T.astype(BF16)
    v_o[...] = _dot(mn, wv_ref[...]).astype(BF16)


def _kv_proj(mem, g_mem, w_kv, *, tn):
    b, n_mem, d = mem.shape
    nj = d // tn
    rows = b * n_mem
    kt, v = pl.pallas_call(
        _kv_kernel,
        grid=(nj,),
        in_specs=[
            _resident((rows, d), lambda j: (0, 0)),
            _resident((1, d), lambda j: (0, 0)),
            pl.BlockSpec((d, tn), lambda j: (0, j)),
            pl.BlockSpec((d, tn), lambda j: (0, nj + j)),
        ],
        out_specs=[pl.BlockSpec((b, tn, n_mem), lambda j: (0, j, 0)),
                   pl.BlockSpec((rows, tn), lambda j: (0, j))],
        out_shape=[jax.ShapeDtypeStruct((b, d, n_mem), BF16), jax.ShapeDtypeStruct((rows, d), BF16)],
        scratch_shapes=[pltpu.VMEM((rows, d), BF16)],
        compiler_params=_params(1),
        name="kv_proj",
    )(mem.reshape(rows, d), g_mem, w_kv, w_kv)
    return kt, v.reshape(b, n_mem, d)


def _xattn_kernel(x_ref, xs_ref, r_ref, gn_ref, wq_ref, kt_ref, v_ref, wxo_ref, o_ref, xs_o, r_o):
    d = x_ref.shape[1]
    hd = d // N_XHEADS
    scale = hd ** -0.5
    cols = [slice(h * hd, (h + 1) * hd) for h in range(N_XHEADS)]
    inv_rms = r_ref[...]

    def query(h):
        return (_dot(xs_ref[...], wq_ref[:, cols[h]]) * inv_rms).astype(BF16)

    def probs(q, h):
        s = _dot(q, kt_ref[cols[h], :]) * scale
        p = jnp.exp(s - jnp.max(s, axis=-1, keepdims=True))
        return (p / jnp.sum(p, axis=-1, keepdims=True)).astype(BF16)

    def attend(p, h):
        return _dot(p, v_ref[:, cols[h]]).astype(BF16)

    def project(o, h):
        return _dot(o, wxo_ref[cols[h], :])

    n = N_XHEADS
    q = [query(h) for h in range(min(2, n))]
    acc = x_ref[...]
    o_prev = None
    for h in range(n):
        p = probs(q[h], h)
        if h + 2 < n:
            q.append(query(h + 2))
        if o_prev is not None:
            acc = acc + project(o_prev, h - 1)
        o_prev = attend(p, h)
    acc = acc + project(o_prev, n - 1)
    o_ref[...] = acc
    xs_o[...], r_o[...] = _split_rms(acc, gn_ref[...])


def _xattn(x, xs, inv_rms, g_next, w_q, kt, v, w_xo, *, tm, casts=()):
    b, s, d = x.shape
    n_mem = v.shape[1]
    tile = pl.BlockSpec((None, tm, d), lambda bi, i: (bi, i, 0))
    col = pl.BlockSpec((None, tm, 1), lambda bi, i: (bi, i, 0))
    return _pallas_call(
        _xattn_kernel,
        grid=(b, s // tm),
        in_specs=[
            tile, tile, col,
            _resident((1, d), lambda bi, i: (0, 0)),
            _resident((d, d), lambda bi, i: (0, 0)),
            pl.BlockSpec((None, d, n_mem), lambda bi, i: (bi, 0, 0)),
            pl.BlockSpec((None, n_mem, d), lambda bi, i: (bi, 0, 0)),
            _resident((d, d), lambda bi, i: (0, 0)),
        ],
        out_specs=[tile, tile, col],
        out_shape=[jax.ShapeDtypeStruct((b, s, d), F32), jax.ShapeDtypeStruct((b, s, d), BF16),
                   jax.ShapeDtypeStruct((b, s, 1), F32)],
        name="xattn",
        casts=casts,
    )(x, xs, inv_rms, g_next, w_q, kt, v, w_xo)


def _ffn_kernel(xs_ref, r_ref, x_hbm, wg_ref, wu_ref, wd_ref, gf_ref, o_ref, x_s, sem):
    tm = xs_ref.shape[0]
    i = pl.program_id(0)
    j = pl.program_id(1)
    fetch_x = pltpu.make_async_copy(x_hbm.at[pl.ds(i * tm, tm), :], x_s, sem)

    last = pl.num_programs(1) - 1

    def hidden(rows=slice(None)):
        xs, inv_rms = xs_ref[rows, :], r_ref[rows, :]
        gt = _dot(xs, wg_ref[...]) * inv_rms
        up = _dot(xs, wu_ref[...]) * inv_rms
        return (gt * jax.nn.sigmoid(gt) * up).astype(BF16)

    def partial_sum():
        return _dot(hidden(), wd_ref[...])

    @pl.when(j == 0)
    def _():
        fetch_x.start()
        o_ref[...] = partial_sum()

    @pl.when(jnp.logical_and(j > 0, j < last))
    def _():
        o_ref[...] += partial_sum()

    @pl.when(j == last)
    def _():
        fetch_x.wait()
        n = FFN_LAST_STEP_CHUNKS
        rows = [slice(c * (tm // n), (c + 1) * (tm // n)) for c in range(n)]
        h = hidden(rows[0])
        for c in range(n):
            h_next = hidden(rows[c + 1]) if c + 1 < n else None
            r = rows[c]
            o_ref[r, :] = _rms(x_s[r, :] + o_ref[r, :] + _dot(h, wd_ref[...]), gf_ref[...])
            h = h_next


def _ffn(xs2d, inv_rms, x2d, w_gate_up, w_down, g_final, *, tm, tf):
    t, d = x2d.shape
    d_ff = w_down.shape[0]
    nj = d_ff // tf
    assert nj >= 2, "first and last d_ff steps must be distinct grid steps"
    tile = pl.BlockSpec((tm, d), lambda i, j: (i, 0))
    return pl.pallas_call(
        _ffn_kernel,
        grid=(t // tm, nj),
        in_specs=[
            tile,
            pl.BlockSpec((tm, 1), lambda i, j: (i, 0)),
            pl.BlockSpec(memory_space=pl.ANY),
            pl.BlockSpec((d, tf), lambda i, j: (0, j)),
            pl.BlockSpec((d, tf), lambda i, j: (0, nj + j)),
            pl.BlockSpec((tf, d), lambda i, j: (j, 0)),
            _resident((1, d), lambda i, j: (0, 0)),
        ],
        out_specs=tile,
        out_shape=jax.ShapeDtypeStruct((t, d), F32),
        scratch_shapes=[pltpu.VMEM((tm, d), F32), pltpu.SemaphoreType.DMA(())],
        compiler_params=_params(2),
        name="ffn",
    )(xs2d, inv_rms, x2d, w_gate_up, w_gate_up, w_down, g_final)


def _tile(n, want):
    for t in range(min(want, n), 15, -1):
        if n % t == 0 and t % 16 == 0:
            return t
    return n


_CAST_IN_PROJ = ("w_out_a", "w_out_b", "w_o", "w_kv")
_CAST_MIXER = ("w_q", "w_xo")
_CAST_XATTN = ("w_gate_up", "w_down")


def _trunk(x, mem, p, f32_weights=None):
    b, s, d = x.shape
    w = d // 2
    t = b * s
    row = lambda a: a.reshape(1, -1)
    p = dict(p)

    def casts(names):
        return [f32_weights[k] for k in names] if f32_weights else []

    (za, cb, sga, sgb), done = _in_proj(x, row(p["g_mix"]), p["w_in"], p["conv_a_w"], p["conv_b_w"],
                                        row(p["conv_b_bias"]), tm=_tile(s, 1024), tn=_tile(w, 256),
                                        casts=casts(_CAST_IN_PROJ))
    p.update(zip(_CAST_IN_PROJ, done))
    (x1, xs1, r1), done = _mixer(x, za, cb, sga, sgb, row(p["ln_b_g"]), row(p["ln_b_b"]),
                                 p["w_out_a"], p["w_out_b"], p["w_o"], row(p["g_xattn"]),
                                 tm=_tile(s, 512), casts=casts(_CAST_MIXER))
    p.update(zip(_CAST_MIXER, done))
    kt, v = _kv_proj(mem, row(p["g_mem"]), p["w_kv"], tn=_tile(d, 512))
    (x2, xs2, r2), done = _xattn(x1, xs1, r1, row(p["g_ffn"]), p["w_q"], kt, v, p["w_xo"],
                                 tm=_tile(s, 512), casts=casts(_CAST_XATTN))
    p.update(zip(_CAST_XATTN, done))
    d_ff = p["w_down"].shape[0]
    y = _ffn(xs2.reshape(t, d), r2.reshape(t, 1), x2.reshape(t, d), p["w_gate_up"], p["w_down"],
             row(p["g_final"]), tm=_tile(t, 1024), tf=_tile(d_ff, 512))
    return y.reshape(b, s, d), p


def kernel(x_prompt, x_sample, mem_prompt, mem_sample, g_mix, w_in, conv_a_w, w_out_a, conv_b_w,
           conv_b_bias, ln_b_g, ln_b_b, w_out_b, w_o, g_xattn, g_mem, w_q, w_kv, w_xo, g_ffn,
           w_gate_up, w_down, g_final):
    assert g_mix.shape[0] == 1, "single-layer trunk"
    p = dict(
        g_mix=g_mix[0], conv_a_w=conv_a_w[0], conv_b_w=conv_b_w[0], conv_b_bias=conv_b_bias[0],
        ln_b_g=ln_b_g[0], ln_b_b=ln_b_b[0], g_xattn=g_xattn[0], g_mem=g_mem[0], g_ffn=g_ffn[0],
        g_final=g_final,
        w_in=w_in[0].astype(BF16),
    )
    f32_weights = dict(w_out_a=w_out_a[0], w_out_b=w_out_b[0], w_o=w_o[0], w_q=w_q[0], w_kv=w_kv[0],
                       w_xo=w_xo[0], w_gate_up=w_gate_up[0], w_down=w_down[0])
    y_prompt, p = _trunk(x_prompt, mem_prompt, p, f32_weights)
    y_sample, _ = _trunk(x_sample, mem_sample, p)
    return (y_prompt, y_sample)
```

```python
import math

import jax
import jax.numpy as jnp
from jax import lax
from jax.experimental import pallas as pl
from jax.experimental.pallas import tpu as pltpu

EPS = 1e-6
N_XHEADS = 4
K_A = 3
K_B = 31
HALO = 16
LANES = 128
SUBLANES = 8
V7X_VMEM_LIMIT_BYTES = 60 * 1024 * 1024

F32 = jnp.float32
BF16 = jnp.bfloat16


def _params(n_grid):
    return pltpu.CompilerParams(
        dimension_semantics=("arbitrary",) * n_grid,
        vmem_limit_bytes=V7X_VMEM_LIMIT_BYTES)


def _resident(shape, index_map):
    return pl.BlockSpec(shape, index_map, pipeline_mode=pl.Buffered(1))


def _rms(x, g):
    return x * lax.rsqrt(jnp.mean(x * x, axis=-1, keepdims=True) + EPS) * g


def _dot(a, b):
    return jnp.dot(a, b, preferred_element_type=F32)


def _split_rms(x, g):
    return (x * g).astype(BF16), lax.rsqrt(jnp.mean(x * x, axis=-1, keepdims=True) + EPS)


def _row_walk_specs(weights, grid):
    n_steps = math.prod(grid)
    strides = [math.prod(grid[k + 1:]) for k in range(len(grid))]
    specs = []
    for wgt in weights:
        r, c = wgt.shape
        n_blocks = max(n for n in range(1, n_steps + 1) if r % n == 0 and (r // n) % 16 == 0)

        def index_map(*g, n_blocks=n_blocks):
            step = sum(gi * st for gi, st in zip(g, strides))
            return (jnp.minimum(step, n_blocks - 1), 0)

        specs.append(pl.BlockSpec((r // n_blocks, c), index_map))
    return specs


def _pallas_call(body, *, grid, in_specs, out_specs, out_shape, scratch_shapes=(), name, casts=()):
    n_in, n_out, n_cast = len(in_specs), len(out_specs), len(casts)
    cast_specs = _row_walk_specs(casts, grid)

    def kern(*refs):
        ins, rest = refs[:n_in], refs[n_in:]
        cast_in, rest = rest[:n_cast], rest[n_cast:]
        outs, rest = rest[:n_out], rest[n_out:]
        cast_out, scratch = rest[:n_cast], rest[n_cast:]
        body(*ins, *outs, *scratch)
        for src, dst in zip(cast_in, cast_out):
            dst[...] = src[...].astype(BF16)

    call = pl.pallas_call(
        kern,
        grid=grid,
        in_specs=[*in_specs, *cast_specs],
        out_specs=[*out_specs, *cast_specs],
        out_shape=[*out_shape, *(jax.ShapeDtypeStruct(c.shape, BF16) for c in casts)],
        scratch_shapes=list(scratch_shapes),
        compiler_params=_params(len(grid)),
        name=name,
    )

    def run(*operands):
        res = call(*operands, *casts)
        return res[:n_out], res[n_out:]

    return run


def _depthwise_conv(ext_s, w_s, out_s, init, n_taps, tm):
    n_slabs = ext_s.shape[0]
    span = 2 * SUBLANES
    off = HALO - n_taps // 2
    for s in range(n_slabs):
        for r0 in range(0, tm, span):
            acc = [init[s], init[s]]
            for k in range(n_taps):
                w = w_s[k, s]
                for ph in range(2):
                    acc[ph] = acc[ph] + w * ext_s[s, pl.ds(r0 + ph + off + k, SUBLANES, stride=2), :]
            for ph in range(2):
                out_s[s, pl.ds(r0 + ph, SUBLANES, stride=2), :] = acc[ph]


def _in_proj_kernel(x_ref, xp_ref, xn_ref, g_ref, wab, wac, wav, wbv, wbg, wga, wgb,
                    caw_ref, cbw_ref, cbb_ref,
                    za_o, cb_o, sga_o, sgb_o,
                    xe_s, exta_s, extb_s, ca_s, cbo_s, wa_s, wb_s):
    tm = x_ref.shape[0]
    n_slabs = exta_s.shape[0]
    i = pl.program_id(1)

    @pl.when(pl.program_id(2) == 0)
    def _():
        g = g_ref[...]
        inside_prev = i > 0
        inside_next = i < pl.num_programs(1) - 1
        xe_s[0:HALO, :] = jnp.where(inside_prev, _rms(xp_ref[...], g), 0.0).astype(BF16)
        xe_s[HALO:HALO + tm, :] = _rms(x_ref[...], g).astype(BF16)
        xe_s[HALO + tm:HALO + tm + HALO, :] = jnp.where(inside_next, _rms(xn_ref[...], g), 0.0).astype(BF16)

    lanes = [slice(s * LANES, (s + 1) * LANES) for s in range(n_slabs)]
    bias = []
    for s in range(n_slabs):
        for k in range(K_A):
            wa_s[k, s] = jnp.broadcast_to(caw_ref[k:k + 1, lanes[s]], (SUBLANES, LANES))
        for k in range(K_B):
            wb_s[k, s] = jnp.broadcast_to(cbw_ref[k:k + 1, lanes[s]], (SUBLANES, LANES))
        bias.append(jnp.broadcast_to(cbb_ref[0:1, lanes[s]], (SUBLANES, LANES)))
    zero = [jnp.zeros((SUBLANES, LANES), F32)] * n_slabs

    xe = xe_s[...]
    xc = xe_s[HALO:HALO + tm, :]
    u = _dot(xe, wbv[...]) * jax.nn.sigmoid(_dot(xe, wbg[...]))
    cv = _dot(xe, wac[...]) * _dot(xe, wav[...])
    for s in range(n_slabs):
        extb_s[s] = u[:, lanes[s]]
        exta_s[s] = cv[:, lanes[s]]
    sga_o[...] = jax.nn.sigmoid(_dot(xc, wga[...])).astype(BF16)
    sgb_o[...] = jax.nn.sigmoid(_dot(xc, wgb[...])).astype(BF16)
    ab = _dot(xc, wab[...])

    _depthwise_conv(extb_s, wb_s, cbo_s, bias, K_B, tm)
    _depthwise_conv(exta_s, wa_s, ca_s, zero, K_A, tm)
    for s in range(n_slabs):
        cb_o[:, lanes[s]] = cbo_s[s]
        za_o[:, lanes[s]] = (ab[:, lanes[s]] * ca_s[s]).astype(BF16)


def _in_proj(x, g_mix, w_in, conv_a_w, conv_b_w, conv_b_bias, *, tm, tn, casts=()):
    b, s, d = x.shape
    w = d // 2
    n_slabs = tn // LANES
    hb = tm // HALO
    n_hb = s // HALO
    grid = (b, s // tm, w // tn)

    def wspec(col0, width):
        blk0 = col0 // width
        return pl.BlockSpec((d, width), lambda bi, i, j: (0, blk0 + j))

    def cspec(rows):
        return pl.BlockSpec((rows, tn), lambda bi, i, j: (0, j))

    in_specs = [
        pl.BlockSpec((None, tm, d), lambda bi, i, j: (bi, i, 0)),
        pl.BlockSpec((None, HALO, d), lambda bi, i, j: (bi, jnp.maximum(i * hb - 1, 0), 0)),
        pl.BlockSpec((None, HALO, d), lambda bi, i, j: (bi, jnp.minimum((i + 1) * hb, n_hb - 1), 0)),
        _resident((1, d), lambda bi, i, j: (0, 0)),
        wspec(0, tn), wspec(w, tn), wspec(2 * w, tn), wspec(3 * w, tn), wspec(4 * w, tn),
        wspec(5 * w, 2 * tn), wspec(7 * w, 2 * tn),
        cspec(K_A), cspec(K_B), cspec(1),
    ]
    nspec = pl.BlockSpec((None, tm, tn), lambda bi, i, j: (bi, i, j))
    wide = pl.BlockSpec((None, tm, 2 * tn), lambda bi, i, j: (bi, i, j))
    out_shape = [jax.ShapeDtypeStruct((b, s, w), BF16), jax.ShapeDtypeStruct((b, s, w), F32),
                 jax.ShapeDtypeStruct((b, s, d), BF16), jax.ShapeDtypeStruct((b, s, d), BF16)]
    ext = pltpu.VMEM((n_slabs, tm + 2 * HALO, LANES), F32)
    out = pltpu.VMEM((n_slabs, tm, LANES), F32)
    return _pallas_call(
        _in_proj_kernel,
        grid=grid,
        in_specs=in_specs,
        out_specs=[nspec, nspec, wide, wide],
        out_shape=out_shape,
        scratch_shapes=[pltpu.VMEM((tm + 2 * HALO, d), BF16), ext, ext, out, out,
                        pltpu.VMEM((K_A, n_slabs, SUBLANES, LANES), F32),
                        pltpu.VMEM((K_B, n_slabs, SUBLANES, LANES), F32)],
        name="in_proj",
        casts=casts,
    )(x, x, x, g_mix, *([w_in] * 7), conv_a_w, conv_b_w, conv_b_bias)


def _mixer_kernel(x_ref, za_ref, cb_ref, sga_ref, sgb_ref, lng_ref, lnb_ref,
                  woa_ref, wob_ref, wo_ref, o_ref):
    cb = cb_ref[...]
    cen = cb - jnp.mean(cb, axis=-1, keepdims=True)
    var = jnp.mean(cen * cen, axis=-1, keepdims=True)
    ln = cen * lax.rsqrt(var + EPS) * lng_ref[...] + lnb_ref[...]
    zb = (ln * jax.nn.sigmoid(ln)).astype(BF16)
    ya = _dot(za_ref[...], woa_ref[...])
    yb = _dot(zb, wob_ref[...])
    merged = (sga_ref[...].astype(F32) * ya + sgb_ref[...].astype(F32) * yb).astype(BF16)
    o_ref[...] = x_ref[...] + _dot(merged, wo_ref[...])


def _mixer(x, za, cb, sga, sgb, ln_g, ln_b, w_out_a, w_out_b, w_o, *, tm, casts=()):
    b, s, d = x.shape
    w = d // 2

    def tile(c):
        return pl.BlockSpec((None, tm, c), lambda bi, i: (bi, i, 0))

    def const(shape):
        return _resident(shape, lambda bi, i: (0,) * len(shape))

    return _pallas_call(
        _mixer_kernel,
        grid=(b, s // tm),
        in_specs=[tile(d), tile(w), tile(w), tile(d), tile(d), const((1, w)), const((1, w)),
                  const((w, d)), const((w, d)), const((d, d))],
        out_specs=[tile(d)],
        out_shape=[jax.ShapeDtypeStruct((b, s, d), F32)],
        name="mixer",
        casts=casts,
    )(x, za, cb, sga, sgb, ln_g, ln_b, w_out_a, w_out_b, w_o)


def _kv_kernel(mem_ref, g_ref, wk_ref, wv_ref, kt_o, v_o, mn_s):
    @pl.when(pl.program_id(0) == 0)
    def _():
        mn_s[...] = _rms(mem_ref[...], g_ref[...]).astype(BF16)

    n_mem = kt_o.shape[2]
    mn = mn_s[...]
    k = _dot(mn, wk_ref[...])
    for bi in range(kt_o.shape[0]):
        kt_o[bi] = k[bi * n_mem:(bi + 1) * n_mem, :].T.astype(BF16)
    v_o[...] = _dot(mn, wv_ref[...]).astype(BF16)


def _kv_proj(mem, g_mem, w_kv, *, tn):
    b, n_mem, d = mem.shape
    nj = d // tn
    rows = b * n_mem
    kt, v = pl.pallas_call(
        _kv_kernel,
        grid=(nj,),
        in_specs=[
            _resident((rows, d), lambda j: (0, 0)),
            _resident((1, d), lambda j: (0, 0)),
            pl.BlockSpec((d, tn), lambda j: (0, j)),
            pl.BlockSpec((d, tn), lambda j: (0, nj + j)),
        ],
        out_specs=[pl.BlockSpec((b, tn, n_mem), lambda j: (0, j, 0)),
                   pl.BlockSpec((rows, tn), lambda j: (0, j))],
        out_shape=[jax.ShapeDtypeStruct((b, d, n_mem), BF16), jax.ShapeDtypeStruct((rows, d), BF16)],
        scratch_shapes=[pltpu.VMEM((rows, d), BF16)],
        compiler_params=_params(1),
        name="kv_proj",
    )(mem.reshape(rows, d), g_mem, w_kv, w_kv)
    return kt, v.reshape(b, n_mem, d)


def _xattn_kernel(x_ref, g_ref, gn_ref, wq_ref, kt_ref, v_ref, wxo_ref, o_ref, xs_o, r_o, hn_s):
    d = x_ref.shape[1]
    hd = d // N_XHEADS
    scale = hd ** -0.5
    cols = [slice(h * hd, (h + 1) * hd) for h in range(N_XHEADS)]
    hn_s[...] = _rms(x_ref[...], g_ref[...]).astype(BF16)

    def query(h):
        return _dot(hn_s[...], wq_ref[:, cols[h]]).astype(BF16)

    def probs(q, h):
        s = _dot(q, kt_ref[cols[h], :]) * scale
        p = jnp.exp(s - jnp.max(s, axis=-1, keepdims=True))
        return (p / jnp.sum(p, axis=-1, keepdims=True)).astype(BF16)

    def attend(p, h):
        return _dot(p, v_ref[:, cols[h]]).astype(BF16)

    def project(o, h):
        return _dot(o, wxo_ref[cols[h], :])

    n = N_XHEADS
    q = [query(h) for h in range(min(2, n))]
    acc = x_ref[...]
    o_prev = None
    for h in range(n):
        p = probs(q[h], h)
        if h + 2 < n:
            q.append(query(h + 2))
        if o_prev is not None:
            acc = acc + project(o_prev, h - 1)
        o_prev = attend(p, h)
    acc = acc + project(o_prev, n - 1)
    o_ref[...] = acc
    xs_o[...], r_o[...] = _split_rms(acc, gn_ref[...])


def _xattn(x, g_xattn, g_next, w_q, kt, v, w_xo, *, tm, casts=()):
    b, s, d = x.shape
    n_mem = v.shape[1]
    tile = pl.BlockSpec((None, tm, d), lambda bi, i: (bi, i, 0))
    col = pl.BlockSpec((None, tm, 1), lambda bi, i: (bi, i, 0))
    return _pallas_call(
        _xattn_kernel,
        grid=(b, s // tm),
        in_specs=[
            tile,
            _resident((1, d), lambda bi, i: (0, 0)),
            _resident((1, d), lambda bi, i: (0, 0)),
            _resident((d, d), lambda bi, i: (0, 0)),
            pl.BlockSpec((None, d, n_mem), lambda bi, i: (bi, 0, 0)),
            pl.BlockSpec((None, n_mem, d), lambda bi, i: (bi, 0, 0)),
            _resident((d, d), lambda bi, i: (0, 0)),
        ],
        out_specs=[tile, tile, col],
        out_shape=[jax.ShapeDtypeStruct((b, s, d), F32), jax.ShapeDtypeStruct((b, s, d), BF16),
                   jax.ShapeDtypeStruct((b, s, 1), F32)],
        scratch_shapes=[pltpu.VMEM((tm, d), BF16)],
        name="xattn",
        casts=casts,
    )(x, g_xattn, g_next, w_q, kt, v, w_xo)


def _ffn_kernel(xs_ref, r_ref, x_hbm, wg_ref, wu_ref, wd_ref, gf_ref, o_ref, x_s, sem):
    tm = xs_ref.shape[0]
    i = pl.program_id(0)
    j = pl.program_id(1)
    fetch_x = pltpu.make_async_copy(x_hbm.at[pl.ds(i * tm, tm), :], x_s, sem)

    last = pl.num_programs(1) - 1

    def partial_sum():
        xs, inv_rms = xs_ref[...], r_ref[...]
        gt = _dot(xs, wg_ref[...]) * inv_rms
        up = _dot(xs, wu_ref[...]) * inv_rms
        return _dot((gt * jax.nn.sigmoid(gt) * up).astype(BF16), wd_ref[...])

    @pl.when(j == 0)
    def _():
        fetch_x.start()
        o_ref[...] = partial_sum()

    @pl.when(jnp.logical_and(j > 0, j < last))
    def _():
        o_ref[...] += partial_sum()

    @pl.when(j == last)
    def _():
        fetch_x.wait()
        y = x_s[...] + o_ref[...] + partial_sum()
        o_ref[...] = y
        ms = jnp.mean(y * y, axis=-1, keepdims=True)
        o_ref[...] = o_ref[...] * lax.rsqrt(ms + EPS) * gf_ref[...]


def _ffn(xs2d, inv_rms, x2d, w_gate_up, w_down, g_final, *, tm, tf):
    t, d = x2d.shape
    d_ff = w_down.shape[0]
    nj = d_ff // tf
    assert nj >= 2, "first and last d_ff steps must be distinct grid steps"
    tile = pl.BlockSpec((tm, d), lambda i, j: (i, 0))
    return pl.pallas_call(
        _ffn_kernel,
        grid=(t // tm, nj),
        in_specs=[
            tile,
            pl.BlockSpec((tm, 1), lambda i, j: (i, 0)),
            pl.BlockSpec(memory_space=pl.ANY),
            pl.BlockSpec((d, tf), lambda i, j: (0, j)),
            pl.BlockSpec((d, tf), lambda i, j: (0, nj + j)),
            pl.BlockSpec((tf, d), lambda i, j: (j, 0)),
            _resident((1, d), lambda i, j: (0, 0)),
        ],
        out_specs=tile,
        out_shape=jax.ShapeDtypeStruct((t, d), F32),
        scratch_shapes=[pltpu.VMEM((tm, d), F32), pltpu.SemaphoreType.DMA(())],
        compiler_params=_params(2),
        name="ffn",
    )(xs2d, inv_rms, x2d, w_gate_up, w_gate_up, w_down, g_final)


def _tile(n, want):
    for t in range(min(want, n), 15, -1):
        if n % t == 0 and t % 16 == 0:
            return t
    return n


_CAST_IN_PROJ = ("w_out_a", "w_out_b", "w_o")
_CAST_MIXER = ("w_q", "w_xo", "w_kv")
_CAST_XATTN = ("w_gate_up", "w_down")


def _trunk(x, mem, p, f32_weights=None):
    b, s, d = x.shape
    w = d // 2
    t = b * s
    row = lambda a: a.reshape(1, -1)
    p = dict(p)

    def casts(names):
        return [f32_weights[k] for k in names] if f32_weights else []

    (za, cb, sga, sgb), done = _in_proj(x, row(p["g_mix"]), p["w_in"], p["conv_a_w"], p["conv_b_w"],
                                        row(p["conv_b_bias"]), tm=_tile(s, 1024), tn=_tile(w, 256),
                                        casts=casts(_CAST_IN_PROJ))
    p.update(zip(_CAST_IN_PROJ, done))
    (x1,), done = _mixer(x, za, cb, sga, sgb, row(p["ln_b_g"]), row(p["ln_b_b"]),
                         p["w_out_a"], p["w_out_b"], p["w_o"], tm=_tile(s, 512), casts=casts(_CAST_MIXER))
    p.update(zip(_CAST_MIXER, done))
    kt, v = _kv_proj(mem, row(p["g_mem"]), p["w_kv"], tn=_tile(d, 512))
    (x2, xs2, r2), done = _xattn(x1, row(p["g_xattn"]), row(p["g_ffn"]), p["w_q"], kt, v, p["w_xo"],
                                 tm=_tile(s, 512), casts=casts(_CAST_XATTN))
    p.update(zip(_CAST_XATTN, done))
    d_ff = p["w_down"].shape[0]
    y = _ffn(xs2.reshape(t, d), r2.reshape(t, 1), x2.reshape(t, d), p["w_gate_up"], p["w_down"],
             row(p["g_final"]), tm=_tile(t, 1024), tf=_tile(d_ff, 512))
    return y.reshape(b, s, d), p


def kernel(x_prompt, x_sample, mem_prompt, mem_sample, g_mix, w_in, conv_a_w, w_out_a, conv_b_w,
           conv_b_bias, ln_b_g, ln_b_b, w_out_b, w_o, g_xattn, g_mem, w_q, w_kv, w_xo, g_ffn,
           w_gate_up, w_down, g_final):
    assert g_mix.shape[0] == 1, "single-layer trunk"
    p = dict(
        g_mix=g_mix[0], conv_a_w=conv_a_w[0], conv_b_w=conv_b_w[0], conv_b_bias=conv_b_bias[0],
        ln_b_g=ln_b_g[0], ln_b_b=ln_b_b[0], g_xattn=g_xattn[0], g_mem=g_mem[0], g_ffn=g_ffn[0],
        g_final=g_final,
        w_in=w_in[0].astype(BF16),
    )
    f32_weights = dict(w_out_a=w_out_a[0], w_out_b=w_out_b[0], w_o=w_o[0], w_q=w_q[0], w_kv=w_kv[0],
                       w_xo=w_xo[0], w_gate_up=w_gate_up[0], w_down=w_down[0])
    y_prompt, p = _trunk(x_prompt, mem_prompt, p, f32_weights)
    y_sample, _ = _trunk(x_sample, mem_sample, p)
    return (y_prompt, y_sample)
```
